```python
import jax, jax.numpy as jnp
from jax import lax
import numpy as np

D_MODEL = 1024
BATCH = 4
SEQ = 8192
DEPTH = 2

CTX_LEN = 256
GRID_W = 64

CONV_DIM = 512
CONV_WIDTH = 31
LRU_DIM = 512
LRU_BLOCKS = 8
LRU_CONV = 4
LRU_C = 8.0
GDN_HEADS = 4
GDN_HEAD_DIM = 128
GDN_DIM = GDN_HEADS * GDN_HEAD_DIM
GDN_CONV = 4
GDN_CHUNK = 64
ATT_HEADS = 8
ATT_KV_HEADS = 2
ATT_HEAD_DIM = 64
ATT_DIM = ATT_HEADS * ATT_HEAD_DIM
ATT_KV_DIM = ATT_KV_HEADS * ATT_HEAD_DIM
ATT_WINDOW = 128
ATT_BLOCK = 128
ROPE_BASE = 10000.0

N_BRANCH = 4
D_FF = 2816
LN_EPS = 1e-5
NORM_EPS = 1e-6
DEEPNORM_ALPHA = (2 * DEPTH) ** 0.25
DEEPNORM_BETA = (8 * DEPTH) ** -0.25

IN_WIDTHS = (CONV_DIM, CONV_DIM, LRU_DIM, LRU_DIM, 3 * GDN_DIM, GDN_DIM, 2 * GDN_HEADS, 2 * GDN_HEADS, ATT_DIM, ATT_KV_DIM, ATT_KV_DIM, N_BRANCH * D_MODEL)
IN_TOTAL = sum(IN_WIDTHS)

kernel_name = 'hybrid_conv_lru_deltanet_swa_flow_block'


def layer_norm(x):
    xf = x.astype(jnp.float32)
    mu = jnp.mean(xf, axis=-1, keepdims=True)
    var = jnp.mean(jnp.square(xf - mu), axis=-1, keepdims=True)
    return ((xf - mu) * lax.rsqrt(var + LN_EPS)).astype(x.dtype)


def modulate(x, shift, scale):
    return layer_norm(x) * (1.0 + scale) + shift


def swiglu(h, w_gate_up, w_down):
    g, u = jnp.split(h @ w_gate_up, 2, axis=-1)
    return (jax.nn.silu(g) * u) @ w_down


def depthwise_conv(x, w, pad_left, pad_right):
    n_ch = w.shape[1]
    return lax.conv_general_dilated(x, w[:, None, :], window_strides=(1,), padding=[(pad_left, pad_right)], dimension_numbers=('NWC', 'WIO', 'NWC'), feature_group_count=n_ch)


def _flip(t):
    return t[:, ::-1]


def _ident(t):
    return t


def split_in(z):
    idx = np.cumsum(IN_WIDTHS)[:-1].tolist()
    return jnp.split(z, idx, axis=-1)


def axial_rope(n_rows):
    row = jnp.repeat(jnp.arange(n_rows, dtype=jnp.float32), GRID_W)
    col = jnp.tile(jnp.arange(GRID_W, dtype=jnp.float32), n_rows)
    n_freq = ATT_HEAD_DIM // 4
    inv = ROPE_BASE ** (-jnp.arange(n_freq, dtype=jnp.float32) / n_freq)
    ang = jnp.concatenate([row[:, None] * inv, col[:, None] * inv], axis=-1)
    return jnp.cos(ang), jnp.sin(ang)


def apply_rope(t, cos, sin):
    half = t.shape[-1] // 2
    t1, t2 = t[..., :half], t[..., half:]
    c = cos[None, :, None, :].astype(t.dtype)
    s = sin[None, :, None, :].astype(t.dtype)
    return jnp.concatenate([t1 * c - t2 * s, t1 * s + t2 * c], axis=-1)


def conformer_conv(val, gate, p):
    u = val * jax.nn.sigmoid(gate)
    half = (CONV_WIDTH - 1) // 2
    u = depthwise_conv(u, p['conv_w'], half, half) + p['conv_b']
    u = layer_norm(u) * p['conv_norm_g'] + p['conv_norm_b']
    return jax.nn.silu(u)


def _linear_combine(left, right):
    a1, b1 = left
    a2, b2 = right
    return a1 * a2, a2 * b1 + b2


def rglru_direction(xs, h0, conv_w, conv_b, w_r, b_r, w_i, b_i, lam):
    xc = depthwise_conv(xs, conv_w, LRU_CONV - 1, 0) + conv_b
    xf = xc.astype(jnp.float32)
    xb = xf.reshape(xf.shape[:-1] + (LRU_BLOCKS, LRU_DIM // LRU_BLOCKS))
    r = jax.nn.sigmoid(jnp.einsum('btnd,nde->btne', xb, w_r).reshape(xf.shape) + b_r)
    i = jax.nn.sigmoid(jnp.einsum('btnd,nde->btne', xb, w_i).reshape(xf.shape) + b_i)
    log_a = -LRU_C * r * jax.nn.softplus(-lam)
    a = jnp.exp(log_a)
    b = jnp.sqrt(-jnp.expm1(2.0 * log_a)) * (i * xf)
    b = b.at[:, 0].add(a[:, 0] * h0)
    _, h = lax.associative_scan(_linear_combine, (a, b), axis=1)
    return h, h[:, -1]


def rglru_branch(x_lat, x_ctx, g_lat, g_ctx, p, ctx_out):
    h0 = jnp.zeros((x_lat.shape[0], LRU_DIM), jnp.float32)
    h_lat, h_ctx = [], []
    for d in range(2):
        fl = _flip if d else _ident
        prm = (p['lru_conv_w'][d], p['lru_conv_b'][d], p['lru_w_r'][d], p['lru_b_r'][d], p['lru_w_i'][d], p['lru_b_i'][d], p['lru_lam'][d])
        hc, s_ctx = rglru_direction(fl(x_ctx), h0, *prm)
        hl, _ = rglru_direction(fl(x_lat), s_ctx, *prm)
        h_lat.append(fl(hl))
        h_ctx.append(fl(hc))
    y = (h_lat[0] + h_lat[1]).astype(x_lat.dtype) * jax.nn.gelu(g_lat)
    y_c = (h_ctx[0] + h_ctx[1]).astype(x_ctx.dtype) * jax.nn.gelu(g_ctx) if ctx_out else None
    return y, y_c


def gdn_chunked(q, k, v, beta, g, s0):
    B_, T, H, dk = q.shape
    C = GDN_CHUNK
    N = T // C

    def chunks(t):
        return jnp.moveaxis(t.reshape((B_, N, C) + t.shape[2:]), 3, 1)

    q, k, v, beta, g = (chunks(t) for t in (q, k, v, beta, g))
    q = q * (dk ** -0.5)
    gc = jnp.cumsum(g, axis=-1)
    tril = jnp.tril(jnp.ones((C, C), dtype=bool))
    eye = jnp.eye(C, dtype=bool)
    decay = jnp.exp(jnp.where(tril, gc[..., :, None] - gc[..., None, :], -jnp.inf))
    kb = k * beta[..., None]
    m = jnp.where(tril & ~eye, jnp.einsum('bhncd,bhnsd->bhncs', kb, k) * decay, 0.0)
    a_mat = m + eye.astype(m.dtype)
    rhs = jnp.concatenate([v * beta[..., None], kb * jnp.exp(gc)[..., None]], axis=-1)
    sol = lax.linalg.triangular_solve(a_mat, rhs, left_side=True, lower=True, unit_diagonal=True)
    dv = v.shape[-1]
    u, w = sol[..., :dv], sol[..., dv:]
    attn = jnp.einsum('bhncd,bhnsd->bhncs', q, k) * decay
    q_dec = q * jnp.exp(gc)[..., None]
    k_state = k * jnp.exp(gc[..., -1:] - gc)[..., None]
    chunk_decay = jnp.exp(gc[..., -1])

    def step(s, inp):
        u_n, w_n, attn_n, qd_n, ks_n, cd_n = inp
        v_new = u_n - jnp.einsum('bhcd,bhde->bhce', w_n, s)
        o = jnp.einsum('bhcd,bhde->bhce', qd_n, s) + jnp.einsum('bhcs,bhse->bhce', attn_n, v_new)
        s = s * cd_n[..., None, None] + jnp.einsum('bhcd,bhce->bhde', ks_n, v_new)
        return s, o

    xs = tuple(jnp.moveaxis(t, 2, 0) for t in (u, w, attn, q_dec, k_state, chunk_decay))
    s_final, o = lax.scan(step, s0, xs)
    o = jnp.transpose(o, (1, 0, 3, 2, 4)).reshape(B_, T, H, dv)
    return o, s_final


def _l2norm(t):
    return t * lax.rsqrt(jnp.sum(jnp.square(t), axis=-1, keepdims=True) + NORM_EPS)


def gdn_direction(qkv, a, b, s0, conv_w, a_log, dt_bias):
    B_, T, _ = qkv.shape
    u = jax.nn.silu(depthwise_conv(qkv, conv_w, GDN_CONV - 1, 0)).astype(jnp.float32)
    q, k, v = (t.reshape(B_, T, GDN_HEADS, GDN_HEAD_DIM) for t in jnp.split(u, 3, axis=-1))
    beta = jax.nn.sigmoid(b.astype(jnp.float32))
    g = -jnp.exp(a_log.astype(jnp.float32)) * jax.nn.softplus(a.astype(jnp.float32) + dt_bias)
    return gdn_chunked(_l2norm(q), _l2norm(k), v, beta, g, s0)


def _gdn_out(o, z, g_norm):
    zz = z.reshape(o.shape).astype(jnp.float32)
    y = o * lax.rsqrt(jnp.mean(jnp.square(o), axis=-1, keepdims=True) + NORM_EPS) * g_norm
    y = y * jax.nn.silu(zz)
    return y.reshape(o.shape[0], o.shape[1], GDN_DIM).astype(z.dtype)


def gdn_branch(qkv, qkv_c, z, z_c, a, a_c, b, b_c, p, ctx_out):
    s0 = jnp.zeros((qkv.shape[0], GDN_HEADS, GDN_HEAD_DIM, GDN_HEAD_DIM), jnp.float32)
    o_lat, o_ctx = [], []
    for d in range(2):
        fl = _flip if d else _ident
        sl = slice(d * GDN_HEADS, (d + 1) * GDN_HEADS)
        prm = (p['gdn_conv_w'][d], p['gdn_a_log'][d], p['gdn_dt_bias'][d])
        oc, s_ctx = gdn_direction(fl(qkv_c), fl(a_c[..., sl]), fl(b_c[..., sl]), s0, *prm)
        ol, _ = gdn_direction(fl(qkv), fl(a[..., sl]), fl(b[..., sl]), s_ctx, *prm)
        o_lat.append(fl(ol))
        o_ctx.append(fl(oc))
    y = _gdn_out(o_lat[0] + o_lat[1], z, p['gdn_norm_g'])
    y_c = _gdn_out(o_ctx[0] + o_ctx[1], z_c, p['gdn_norm_g']) if ctx_out else None
    return y, y_c


def _attend(qg, keys, values, masks, sink):
    n_kv, n_grp = qg.shape[2], qg.shape[3]
    scale = qg.shape[-1] ** -0.5
    logits = []
    for kk, mm in zip(keys, masks):
        s = jnp.einsum('bqhgd,bkhd->bhgqk', qg, kk).astype(jnp.float32) * scale
        if mm is not None:
            s = jnp.where(mm, s, -jnp.inf)
        logits.append(s)
    sink_col = jnp.broadcast_to(sink.astype(jnp.float32).reshape(n_kv, n_grp, 1, 1), logits[0].shape[:-1] + (1,))
    probs = jax.nn.softmax(jnp.concatenate(logits + [sink_col], axis=-1), axis=-1)
    out = None
    off = 0
    for vv in values:
        n = vv.shape[1]
        o = jnp.einsum('bhgqk,bkhd->bqhgd', probs[..., off:off + n].astype(vv.dtype), vv)
        out = o if out is None else out + o
        off += n
    return out


def window_attention(q, k, v, kc, vc, sink):
    B_, T, Hq, d = q.shape
    n_kv = k.shape[2]
    nb = T // ATT_BLOCK
    span = 3 * ATT_BLOCK
    qb = jnp.moveaxis(q.reshape(B_, nb, ATT_BLOCK, n_kv, Hq // n_kv, d), 1, 0)
    pad = ((0, 0), (ATT_BLOCK, ATT_BLOCK), (0, 0), (0, 0))
    kp = jnp.pad(k, pad)
    vp = jnp.pad(v, pad)
    rel = (jnp.arange(span) - ATT_BLOCK)[None, :] - jnp.arange(ATT_BLOCK)[:, None]
    band = jnp.abs(rel) <= ATT_WINDOW

    def block(args):
        qi, i = args
        start = i * ATT_BLOCK
        ki = lax.dynamic_slice_in_dim(kp, start, span, axis=1)
        vi = lax.dynamic_slice_in_dim(vp, start, span, axis=1)
        kpos = start - ATT_BLOCK + jnp.arange(span)
        mask = band & ((kpos >= 0) & (kpos < T))[None, :]
        return _attend(qi, (ki, kc), (vi, vc), (mask, None), sink)

    o = lax.map(block, (qb, jnp.arange(nb)))
    return jnp.moveaxis(o, 0, 1).reshape(B_, T, Hq * d)


def context_attention(qc, kc, vc, sink):
    B_, Lc, Hq, d = qc.shape
    n_kv = kc.shape[2]
    qg = qc.reshape(B_, Lc, n_kv, Hq // n_kv, d)
    return _attend(qg, (kc,), (vc,), (None,), sink).reshape(B_, Lc, Hq * d)


def token_mixer(h, hc, rope, p, ctx_out):
    B_, T, D = h.shape
    Lc = hc.shape[1]
    (ca, cg, lx, lg, gqkv, gz, ga, gb, aq, ak, av, gates) = split_in(h @ p['w_in'])
    (ca_c, cg_c, lx_c, lg_c, gqkv_c, gz_c, ga_c, gb_c, aq_c, ak_c, av_c, gates_c) = split_in(hc @ p['w_in'])
    cos, sin = rope

    y_conv = conformer_conv(ca, cg, p)
    y_lru, y_lru_c = rglru_branch(lx, lx_c, lg, lg_c, p, ctx_out)
    y_gdn, y_gdn_c = gdn_branch(gqkv, gqkv_c, gz, gz_c, ga, ga_c, gb, gb_c, p, ctx_out)
    q = apply_rope(aq.reshape(B_, T, ATT_HEADS, ATT_HEAD_DIM), cos, sin)
    k = apply_rope(ak.reshape(B_, T, ATT_KV_HEADS, ATT_HEAD_DIM), cos, sin)
    v = av.reshape(B_, T, ATT_KV_HEADS, ATT_HEAD_DIM)
    kc = ak_c.reshape(B_, Lc, ATT_KV_HEADS, ATT_HEAD_DIM)
    vc = av_c.reshape(B_, Lc, ATT_KV_HEADS, ATT_HEAD_DIM)
    y_att = window_attention(q, k, v, kc, vc, p['att_sink'])

    def merge(y_a, y_b, y_c, y_d, gate_cols):
        gt = jax.nn.sigmoid(gate_cols).reshape(gate_cols.shape[:-1] + (N_BRANCH, D))
        mrg = gt[..., 0, :] * (y_a @ p['proj_conv'])
        mrg = mrg + gt[..., 1, :] * (y_b @ p['proj_lru'])
        mrg = mrg + gt[..., 2, :] * (y_c @ p['proj_gdn'])
        mrg = mrg + gt[..., 3, :] * (y_d @ p['proj_att'])
        return mrg @ p['w_out']

    y = merge(y_conv, y_lru, y_gdn, y_att, gates)
    if not ctx_out:
        return y, None
    y_conv_c = conformer_conv(ca_c, cg_c, p)
    y_att_c = context_attention(aq_c.reshape(B_, Lc, ATT_HEADS, ATT_HEAD_DIM), kc, vc, p['att_sink'])
    yc = merge(y_conv_c, y_lru_c, y_gdn_c, y_att_c, gates_c)
    return y, yc


def trunk_layer(x, xc, mod, mod_c, rope, p, last):
    def post_norm(s, out, j):
        return layer_norm(DEEPNORM_ALPHA * s + out) * p['norm_g'][j] + p['norm_b'][j]

    def ffn_step(s, m, j, f):
        hm = modulate(s, m[:, :, j, 0], m[:, :, j, 1])
        y = swiglu(hm, p['ffn_w_in'][f], p['ffn_w_out'][f])
        return post_norm(s, 0.5 * m[:, :, j, 2] * y, j)

    x = ffn_step(x, mod, 0, 0)
    xc = ffn_step(xc, mod_c, 0, 0)
    h = modulate(x, mod[:, :, 1, 0], mod[:, :, 1, 1])
    hc = modulate(xc, mod_c[:, :, 1, 0], mod_c[:, :, 1, 1])
    y, yc = token_mixer(h, hc, rope, p, not last)
    x = post_norm(x, mod[:, :, 1, 2] * y, 1)
    x = ffn_step(x, mod, 2, 1)
    if not last:
        xc = post_norm(xc, mod_c[:, :, 1, 2] * yc, 1)
        xc = ffn_step(xc, mod_c, 2, 1)
    return x, xc


def setup_inputs(seed: int = 0) -> dict:
    key = jax.random.key(seed)
    ks = iter(jax.random.split(key, 48))
    L = DEPTH
    D = D_MODEL
    bd = LRU_DIM // LRU_BLOCKS

    def nrm(shape, scale):
        return jax.random.normal(next(ks), shape, jnp.float32) * scale

    def unif(shape, lo, hi):
        return jax.random.uniform(next(ks), shape, jnp.float32, minval=lo, maxval=hi)

    a_pow = unif((L, 2, LRU_DIM), 0.9, 0.999) ** (1.0 / LRU_C)
    dt = jnp.exp(unif((L, 2, GDN_HEADS), float(np.log(1e-3)), float(np.log(1e-1))))
    return {
        'x': nrm((BATCH, SEQ, D), 1.0),
        'c': nrm((BATCH, D), 1.0),
        'ctx': nrm((BATCH, CTX_LEN, D), 1.0),
        'c_ctx': nrm((D,), 1.0),
        'ada_w': nrm((L, D, 9 * D), D ** -0.5),
        'ada_b': nrm((L, 9 * D), 0.02),
        'norm_g': 1.0 + nrm((L, 3, D), 0.02),
        'norm_b': nrm((L, 3, D), 0.02),
        'ffn_w_in': nrm((L, 2, D, 2 * D_FF), D ** -0.5),
        'ffn_w_out': nrm((L, 2, D_FF, D), DEEPNORM_BETA * D_FF ** -0.5),
        'w_in': nrm((L, D, IN_TOTAL), D ** -0.5),
        'conv_w': nrm((L, CONV_WIDTH, CONV_DIM), CONV_WIDTH ** -0.5),
        'conv_b': nrm((L, CONV_DIM), 0.02),
        'conv_norm_g': 1.0 + nrm((L, CONV_DIM), 0.02),
        'conv_norm_b': nrm((L, CONV_DIM), 0.02),
        'lru_conv_w': nrm((L, 2, LRU_CONV, LRU_DIM), LRU_CONV ** -0.5),
        'lru_conv_b': nrm((L, 2, LRU_DIM), 0.02),
        'lru_w_r': nrm((L, 2, LRU_BLOCKS, bd, bd), bd ** -0.5),
        'lru_b_r': nrm((L, 2, LRU_DIM), 0.02),
        'lru_w_i': nrm((L, 2, LRU_BLOCKS, bd, bd), bd ** -0.5),
        'lru_b_i': nrm((L, 2, LRU_DIM), 0.02),
        'lru_lam': jnp.log(a_pow) - jnp.log1p(-a_pow),
        'gdn_conv_w': nrm((L, 2, GDN_CONV, 3 * GDN_DIM), GDN_CONV ** -0.5),
        'gdn_a_log': jnp.log(unif((L, 2, GDN_HEADS), 1.0, 16.0)),
        'gdn_dt_bias': dt + jnp.log(-jnp.expm1(-dt)),
        'gdn_norm_g': 1.0 + nrm((L, GDN_HEAD_DIM), 0.02),
        'att_sink': nrm((L, ATT_HEADS), 0.5),
        'proj_conv': nrm((L, CONV_DIM, D), DEEPNORM_BETA * CONV_DIM ** -0.5),
        'proj_lru': nrm((L, LRU_DIM, D), DEEPNORM_BETA * LRU_DIM ** -0.5),
        'proj_gdn': nrm((L, GDN_DIM, D), DEEPNORM_BETA * GDN_DIM ** -0.5),
        'proj_att': nrm((L, ATT_DIM, D), DEEPNORM_BETA * ATT_DIM ** -0.5),
        'w_out': nrm((L, D, D), DEEPNORM_BETA * D ** -0.5),
    }


def reference(x, c, ctx, c_ctx, ada_w, ada_b, norm_g, norm_b, ffn_w_in, ffn_w_out, w_in, conv_w, conv_b, conv_norm_g, conv_norm_b, lru_conv_w, lru_conv_b, lru_w_r, lru_b_r, lru_w_i, lru_b_i, lru_lam, gdn_conv_w, gdn_a_log, gdn_dt_bias, gdn_norm_g, att_sink, proj_conv, proj_lru, proj_gdn, proj_att, w_out):
    B_, T, D = x.shape
    n_rows = T // GRID_W
    rope = axial_rope(n_rows)
    xc = ctx
    for l in range(DEPTH):
        p = dict(norm_g=norm_g[l], norm_b=norm_b[l], ffn_w_in=ffn_w_in[l], ffn_w_out=ffn_w_out[l], w_in=w_in[l], conv_w=conv_w[l], conv_b=conv_b[l], conv_norm_g=conv_norm_g[l], conv_norm_b=conv_norm_b[l], lru_conv_w=lru_conv_w[l], lru_conv_b=lru_conv_b[l], lru_w_r=lru_w_r[l], lru_b_r=lru_b_r[l], lru_w_i=lru_w_i[l], lru_b_i=lru_b_i[l], lru_lam=lru_lam[l], gdn_conv_w=gdn_conv_w[l], gdn_a_log=gdn_a_log[l], gdn_dt_bias=gdn_dt_bias[l], gdn_norm_g=gdn_norm_g[l], att_sink=att_sink[l], proj_conv=proj_conv[l], proj_lru=proj_lru[l], proj_gdn=proj_gdn[l], proj_att=proj_att[l], w_out=w_out[l])
        mod = (jax.nn.silu(c) @ ada_w[l] + ada_b[l]).reshape(B_, 1, 3, 3, D)
        mod_c = (jax.nn.silu(c_ctx) @ ada_w[l] + ada_b[l]).reshape(1, 1, 3, 3, D)
        x, xc = trunk_layer(x, xc, mod, mod_c, rope, p, l == DEPTH - 1)
    return x
```

```python
import functools

import numpy as np
import jax
import jax.numpy as jnp
from jax import lax
from jax.experimental import pallas as pl
from jax.experimental.pallas import tpu as pltpu

F32 = jnp.float32
BF16 = jnp.bfloat16

D_MODEL = 1024
DEPTH = 2
GRID_W = 64
CONV_DIM = 512
CONV_WIDTH = 31
LRU_DIM = 512
LRU_BLOCKS = 8
LRU_CONV = 4
LRU_C = 8.0
GDN_HEADS = 4
GDN_HEAD_DIM = 128
GDN_DIM = GDN_HEADS * GDN_HEAD_DIM
GDN_CONV = 4
GDN_CHUNK = 64
ATT_HEADS = 8
ATT_KV_HEADS = 2
ATT_HEAD_DIM = 64
ATT_DIM = ATT_HEADS * ATT_HEAD_DIM
ATT_BLOCK = 128
ROPE_BASE = 10000.0
N_BRANCH = 4
D_FF = 2816
LN_EPS = 1e-5
NORM_EPS = 1e-6
DEEPNORM_ALPHA = (2 * DEPTH) ** 0.25

CTX_MOD_ROW = 4
MOD_ROWS = 8
FF_CHUNKS = 2
SUBLANES = 8
CONV_HALO = 16
VMEM_LIMIT = 56 * 1024 * 1024
NEG_BIG = -1e30

_OFF = np.cumsum([0, CONV_DIM, CONV_DIM, LRU_DIM, LRU_DIM, 3 * GDN_DIM, GDN_DIM, 2 * GDN_HEADS, 2 * GDN_HEADS,
                  ATT_DIM, ATT_KV_HEADS * ATT_HEAD_DIM, ATT_KV_HEADS * ATT_HEAD_DIM, N_BRANCH * D_MODEL]).tolist()


def _cparams(*sem):
    return pltpu.CompilerParams(dimension_semantics=sem, vmem_limit_bytes=VMEM_LIMIT)


def _resident(shape):
    nd = len(shape)
    return pl.BlockSpec(tuple(shape), lambda *_: (0,) * nd)


def _ln(x):
    mu = jnp.mean(x, axis=-1, keepdims=True)
    xc = x - mu
    var = jnp.mean(xc * xc, axis=-1, keepdims=True)
    return xc * lax.rsqrt(var + LN_EPS)


def _silu(x):
    return x * jax.nn.sigmoid(x)


def _softplus(x):
    return jnp.maximum(x, 0.0) + jnp.log1p(jnp.exp(-jnp.abs(x)))


def _dot(a, b):
    return jnp.dot(a, b, preferred_element_type=F32)


def _dot_nt(a, b):
    return lax.dot_general(a, b, (((1,), (1,)), ((), ())), preferred_element_type=F32)


def _dot_tn(a, b):
    return lax.dot_general(a, b, (((0,), (0,)), ((), ())), preferred_element_type=F32)


def _dot_f32(a, b):
    return jnp.dot(a, b, preferred_element_type=F32, precision=lax.Precision.HIGHEST)


def _mod_spec(ctx):
    if ctx:
        return pl.BlockSpec((1, 9, D_MODEL), lambda b, i: (CTX_MOD_ROW, 0, 0))
    return pl.BlockSpec((1, 9, D_MODEL), lambda b, i: (b, 0, 0))


def _tok_spec(tm, width):
    return pl.BlockSpec((1, tm, width), lambda b, i: (b, i, 0))


def _mod_kernel(c_ref, w_ref, b_ref, o_ref):
    s = _silu(c_ref[...]).astype(BF16)
    o_ref[0] = _dot(s, w_ref[0].astype(BF16)) + b_ref[0]


def _modulation(c_all, ada_w, ada_b):
    n_l, d, n = ada_w.shape
    tn = 1024
    out = pl.pallas_call(
        _mod_kernel,
        out_shape=jax.ShapeDtypeStruct((n_l, MOD_ROWS, n), F32),
        grid=(n_l, n // tn),
        in_specs=[pl.BlockSpec((MOD_ROWS, d), lambda l, j: (0, 0)),
                  pl.BlockSpec((1, d, tn), lambda l, j: (l, 0, j)),
                  pl.BlockSpec((1, 1, tn), lambda l, j: (l, 0, j))],
        out_specs=pl.BlockSpec((1, MOD_ROWS, tn), lambda l, j: (l, 0, j)),
        compiler_params=_cparams("arbitrary", "arbitrary"),
        name="adaln_mod",
    )(c_all, ada_w, ada_b.reshape(n_l, 1, n))
    return out.reshape(n_l, MOD_ROWS, 9, d)


def _ffn_kernel(x_ref, mod_ref, wg_ref, wu_ref, wd_ref, g_ref, b_ref, o_ref, *, j):
    x = x_ref[0]
    m = mod_ref[0]
    shift, scale, gate = m[3 * j:3 * j + 1], m[3 * j + 1:3 * j + 2], m[3 * j + 2:3 * j + 3]
    h = (_ln(x) * (1.0 + scale) + shift).astype(BF16)
    acc = None
    for f in range(FF_CHUNKS):
        g = _dot(h, wg_ref[f])
        u = _dot(h, wu_ref[f])
        a = (_silu(g) * u).astype(BF16)
        y = _dot(a, wd_ref[f])
        acc = y if acc is None else acc + y
    s = DEEPNORM_ALPHA * x + (0.5 * gate) * acc
    o_ref[0] = _ln(s) * g_ref[...] + b_ref[...]


def _ffn(x, mod3, wg, wu, wd, ng, nb, *, j, ctx, tm):
    b_, t, d = x.shape
    return pl.pallas_call(
        functools.partial(_ffn_kernel, j=j),
        out_shape=jax.ShapeDtypeStruct(x.shape, F32),
        grid=(b_, t // tm),
        in_specs=[_tok_spec(tm, d), _mod_spec(ctx), _resident(wg.shape), _resident(wu.shape), _resident(wd.shape),
                  _resident(ng.shape), _resident(nb.shape)],
        out_specs=_tok_spec(tm, d),
        compiler_params=_cparams("parallel", "parallel"),
        name="ffn_ctx" if ctx else "ffn",
    )(x, mod3, wg, wu, wd, ng, nb)


def _rope128(t, cos, sin_signed, lane_lo):
    swapped = jnp.where(lane_lo, pltpu.roll(t, 96, 1), pltpu.roll(t, 32, 1))
    return t * cos + swapped * sin_signed


def _inproj_kernel(*refs, rope):
    if rope:
        x_ref, mod_ref, w_ref, cos_ref, sin_ref, o_conv, o_lru, o_gdn, o_gab, o_q, o_k, o_v = refs
    else:
        x_ref, mod_ref, w_ref, o_conv, o_lru, o_gdn, o_gab, o_q, o_k, o_v = refs
    m = mod_ref[0]
    h = (_ln(x_ref[0]) * (1.0 + m[4:5]) + m[3:4]).astype(BF16)
    o_conv[0] = _dot(h, w_ref[:, 0:1024])
    o_lru[0] = _dot(h, w_ref[:, 1024:2048])
    o_gdn[0] = _dot(h, w_ref[:, 2048:4096])
    o_gab[0] = _dot(h, w_ref[:, 4096:4224])
    q = _dot(h, w_ref[:, 4224:4736])
    k = _dot(h, w_ref[:, 4736:4992])
    o_v[0] = _dot(h, w_ref[:, 4992:5248])
    scale = ATT_HEAD_DIM ** -0.5
    if rope:
        cos, sin = cos_ref[...], sin_ref[...]
        lane_lo = (lax.broadcasted_iota(jnp.int32, cos.shape, 1) % ATT_HEAD_DIM) < (ATT_HEAD_DIM // 2)
        for s in range(ATT_DIM // 128):
            o_q[0, :, 128 * s:128 * (s + 1)] = _rope128(q[:, 128 * s:128 * (s + 1)], cos, sin, lane_lo) * scale
        for s in range(2):
            o_k[0, :, 128 * s:128 * (s + 1)] = _rope128(k[:, 128 * s:128 * (s + 1)], cos, sin, lane_lo)
    else:
        o_q[0] = q * scale
        o_k[0] = k


_INPROJ_WIDTHS = (1024, 1024, 2048, 128, 512, 256, 256)


def _inproj(x, mod3, w_cat, rope_tabs, *, ctx, tm):
    b_, t, d = x.shape
    in_specs = [_tok_spec(tm, d), _mod_spec(ctx), _resident(w_cat.shape)]
    args = [x, mod3, w_cat]
    if rope_tabs is not None:
        in_specs += [pl.BlockSpec((tm, 128), lambda b, i: (i, 0))] * 2
        args += list(rope_tabs)
    return pl.pallas_call(
        functools.partial(_inproj_kernel, rope=rope_tabs is not None),
        out_shape=[jax.ShapeDtypeStruct((b_, t, w), F32) for w in _INPROJ_WIDTHS],
        grid=(b_, t // tm),
        in_specs=in_specs,
        out_specs=[_tok_spec(tm, w) for w in _INPROJ_WIDTHS],
        compiler_params=_cparams("parallel", "parallel"),
        name="inproj_ctx" if ctx else "inproj",
    )(*args)


def _conv_kernel(main_ref, prev_ref, next_ref, w_ref, cb_ref, g_ref, b_ref, o_ref, ubuf, *, tt, n_t, rb):
    i = pl.program_id(1)

    def glu(blk):
        return blk[:, :CONV_DIM] * jax.nn.sigmoid(blk[:, CONV_DIM:])

    ubuf[0:CONV_HALO] = jnp.where(i > 0, glu(prev_ref[0]), 0.0)
    ubuf[CONV_HALO:CONV_HALO + tt] = glu(main_ref[0])
    ubuf[CONV_HALO + tt:] = jnp.where(i < n_t - 1, glu(next_ref[0]), 0.0)
    base = CONV_HALO - (CONV_WIDTH - 1) // 2
    for r0 in range(0, tt, rb):
        acc = None
        for k in range(CONV_WIDTH):
            term = ubuf[base + r0 + k:base + r0 + k + rb, :] * w_ref[k:k + 1, :]
            acc = term if acc is None else acc + term
        y = _ln(acc + cb_ref[...]) * g_ref[...] + b_ref[...]
        o_ref[0, r0:r0 + rb, :] = _silu(y)


def _conformer(conv_in, w, cb, ng, nb, *, tt):
    b_, t, width = conv_in.shape
    n_t = t // tt
    hb = tt // CONV_HALO
    n_h = t // CONV_HALO
    return pl.pallas_call(
        functools.partial(_conv_kernel, tt=tt, n_t=n_t, rb=16),
        out_shape=jax.ShapeDtypeStruct((b_, t, CONV_DIM), F32),
        grid=(b_, n_t),
        in_specs=[_tok_spec(tt, width),
                  pl.BlockSpec((1, CONV_HALO, width), lambda b, i: (b, jnp.maximum(i * hb - 1, 0), 0)),
                  pl.BlockSpec((1, CONV_HALO, width), lambda b, i: (b, jnp.minimum((i + 1) * hb, n_h - 1), 0)),
                  _resident(w.shape), _resident(cb.shape), _resident(ng.shape), _resident(nb.shape)],
        out_specs=_tok_spec(tt, CONV_DIM),
        scratch_shapes=[pltpu.VMEM((tt + 2 * CONV_HALO, CONV_DIM), F32)],
        compiler_params=_cparams("parallel", "parallel"),
        name="conformer_conv",
    )(conv_in, conv_in, conv_in, w, cb, ng, nb)


def _halo_specs(tt, t, width, reverse):
    n_t = t // tt
    hb = tt // SUBLANES
    n_h = t // SUBLANES
    if reverse:
        main = pl.BlockSpec((1, tt, width), lambda b, i: (b, n_t - 1 - i, 0))
        halo = pl.BlockSpec((1, SUBLANES, width), lambda b, i: (b, jnp.minimum((n_t - i) * hb, n_h - 1), 0))
    else:
        main = pl.BlockSpec((1, tt, width), lambda b, i: (b, i, 0))
        halo = pl.BlockSpec((1, SUBLANES, width), lambda b, i: (b, jnp.maximum(i * hb - 1, 0), 0))
    return main, halo


def _fill_conv_buffer(xbuf, main, halo, first, tt, reverse):
    halo = jnp.where(first, 0.0, halo)
    if reverse:
        xbuf[0:tt] = main
        xbuf[tt:] = halo
        return lambda k: LRU_CONV - 1 - k
    xbuf[0:SUBLANES] = halo
    xbuf[SUBLANES:] = main
    return lambda k: SUBLANES - (LRU_CONV - 1) + k


def _lru_kernel(*refs, tt, reverse, final):
    if final:
        (main_ref, halo_ref, h0_ref, cw_ref, cb_ref, wr_ref, br_ref, wi_ref, bi_ref, lam_ref, hf_ref,
         o_ref, st_ref, xbuf, abuf, bbuf, carry) = refs
    else:
        (main_ref, halo_ref, h0_ref, cw_ref, cb_ref, wr_ref, br_ref, wi_ref, bi_ref, lam_ref,
         o_ref, st_ref, xbuf, abuf, bbuf, carry) = refs
    i = pl.program_id(1)

    @pl.when(i == 0)
    def _():
        carry[...] = h0_ref[0]

    off = _fill_conv_buffer(xbuf, main_ref[0, :, 0:LRU_DIM], halo_ref[0, :, 0:LRU_DIM], i == 0, tt, reverse)
    xc = cb_ref[...]
    for k in range(LRU_CONV):
        xc = xc + xbuf[off(k):off(k) + tt, :] * cw_ref[k:k + 1, :]
    xh = xc.astype(BF16)
    half = LRU_DIM // 2
    r_lin = jnp.concatenate([_dot(xh[:, :half], wr_ref[0]), _dot(xh[:, half:], wr_ref[1])], axis=1)
    i_lin = jnp.concatenate([_dot(xh[:, :half], wi_ref[0]), _dot(xh[:, half:], wi_ref[1])], axis=1)
    r = jax.nn.sigmoid(r_lin + br_ref[...])
    gi = jax.nn.sigmoid(i_lin + bi_ref[...])
    log_a = (-LRU_C) * r * _softplus(-lam_ref[...])
    a = jnp.exp(log_a)
    th = jnp.tanh(log_a)
    bb = jnp.sqrt(-2.0 * th / (1.0 - th)) * (gi * xc)

    n_g = tt // SUBLANES
    a3 = a.reshape(n_g, SUBLANES, LRU_DIM)
    b3 = bb.reshape(n_g, SUBLANES, LRU_DIM)
    row = lax.broadcasted_iota(jnp.int32, a3.shape, 1)
    for s in (1, 2, 4):
        if reverse:
            keep = row < SUBLANES - s
            a_sh = pltpu.roll(a3, SUBLANES - s, 1)
            b_sh = pltpu.roll(b3, SUBLANES - s, 1)
        else:
            keep = row >= s
            a_sh = pltpu.roll(a3, s, 1)
            b_sh = pltpu.roll(b3, s, 1)
        b3 = jnp.where(keep, a3 * b_sh + b3, b3)
        a3 = jnp.where(keep, a3 * a_sh, a3)
    abuf[...] = a3.reshape(tt, LRU_DIM)
    bbuf[...] = b3.reshape(tt, LRU_DIM)

    hprev = carry[...]
    order = range(n_g - 1, -1, -1) if reverse else range(n_g)
    edge = 0 if reverse else SUBLANES - 1
    for g in order:
        rows = slice(g * SUBLANES, (g + 1) * SUBLANES)
        hg = abuf[rows, :] * hprev + bbuf[rows, :]
        hprev = hg[edge:edge + 1, :]
        if final:
            gate = jax.nn.gelu(main_ref[0, rows, LRU_DIM:2 * LRU_DIM])
            o_ref[0, rows, :] = (hf_ref[0, rows, :] + hg) * gate
        else:
            o_ref[0, rows, :] = hg
    carry[...] = hprev
    st_ref[0] = hprev


def _lru(lru_in, h0, prm, h_fwd, *, tt, reverse):
    b_, t, width = lru_in.shape
    final = h_fwd is not None
    main, halo = _halo_specs(tt, t, width, reverse)
    in_specs = [main, halo, pl.BlockSpec((1, 1, LRU_DIM), lambda b, i: (b, 0, 0))] + [_resident(p.shape) for p in prm]
    args = [lru_in, lru_in, h0] + list(prm)
    out_tile = pl.BlockSpec((1, tt, LRU_DIM), main.index_map)
    if final:
        in_specs.append(out_tile)
        args.append(h_fwd)
    return pl.pallas_call(
        functools.partial(_lru_kernel, tt=tt, reverse=reverse, final=final),
        out_shape=[jax.ShapeDtypeStruct((b_, t, LRU_DIM), F32), jax.ShapeDtypeStruct((b_, 1, LRU_DIM), F32)],
        grid=(b_, t // tt),
        in_specs=in_specs,
        out_specs=[out_tile, pl.BlockSpec((1, 1, LRU_DIM), lambda b, i: (b, 0, 0))],
        scratch_shapes=[pltpu.VMEM((tt + SUBLANES, LRU_DIM), F32), pltpu.VMEM((tt, LRU_DIM), F32),
                        pltpu.VMEM((tt, LRU_DIM), F32), pltpu.VMEM((1, LRU_DIM), F32)],
        compiler_params=_cparams("parallel", "arbitrary"),
        name="rglru_rev" if reverse else "rglru_fwd",
    )(*args)


def _unit_tri_inverse(m, ri, ci, eye):
    def same_block(log2_size):
        return jnp.right_shift(ri, log2_size) == jnp.right_shift(ci, log2_size)

    p = jnp.where(same_block(3), -m, 0.0)
    x = eye + p
    for _ in range(2):
        p = _dot_f32(p, p)
        x = x + _dot_f32(x, p)
    for log2_size in (3, 4, 5):
        off_diag = jnp.where(same_block(log2_size + 1) & ~same_block(log2_size), m, 0.0)
        x = x - _dot_f32(_dot_f32(x, off_diag), x)
    return x


def _gdn_kernel(*refs, tt, direction, final):
    if final:
        (main_ref, halo_ref, gab_ref, s0_ref, cw_ref, alog_ref, dtb_ref, of_ref, gn_ref,
         o_ref, sfin_ref, xbuf, s_ref) = refs
    else:
        (main_ref, halo_ref, gab_ref, s0_ref, cw_ref, alog_ref, dtb_ref,
         o_ref, sfin_ref, xbuf, s_ref) = refs
    reverse = direction == 1
    i = pl.program_id(1)
    c_ = GDN_CHUNK
    hd = GDN_HEAD_DIM
    qkv_w = 3 * GDN_DIM

    @pl.when(i == 0)
    def _():
        s_ref[...] = s0_ref[0]

    off = _fill_conv_buffer(xbuf, main_ref[0, :, 0:qkv_w], halo_ref[0, :, 0:qkv_w], i == 0, tt, reverse)

    ri = lax.broadcasted_iota(jnp.int32, (c_, c_), 0)
    ci = lax.broadcasted_iota(jnp.int32, (c_, c_), 1)
    incl = (ci >= ri) if reverse else (ci <= ri)
    strict = (ci > ri) if reverse else (ci < ri)
    eye = (ri == ci).astype(F32)
    ones_tri = incl.astype(F32)
    last = 0 if reverse else c_ - 1

    n_c = tt // c_
    for c in (range(n_c - 1, -1, -1) if reverse else range(n_c)):
        r0 = c * c_
        acc = None
        for k in range(GDN_CONV):
            term = xbuf[r0 + off(k):r0 + off(k) + c_, :] * cw_ref[k:k + 1, :]
            acc = term if acc is None else acc + term
        u = _silu(acc)
        ga = gab_ref[0, r0:r0 + c_, :]
        g_all = -jnp.exp(alog_ref[...]) * _softplus(ga + dtb_ref[...])
        beta_all = jax.nn.sigmoid(ga)
        gcum = _dot_f32(ones_tri, g_all)
        gcum_t = gcum.T
        g_last = gcum[last:last + 1, :]
        for h in range(GDN_HEADS):
            col = direction * GDN_HEADS + h
            q = u[:, h * hd:(h + 1) * hd]
            k = u[:, GDN_DIM + h * hd:GDN_DIM + (h + 1) * hd]
            v = u[:, 2 * GDN_DIM + h * hd:2 * GDN_DIM + (h + 1) * hd]
            q = q * (lax.rsqrt(jnp.sum(q * q, axis=-1, keepdims=True) + NORM_EPS) * (hd ** -0.5))
            k = k * lax.rsqrt(jnp.sum(k * k, axis=-1, keepdims=True) + NORM_EPS)
            gc = gcum[:, col:col + 1]
            gr = gcum_t[col:col + 1, :]
            gl = g_last[:, col:col + 1]
            beta = beta_all[:, 2 * GDN_HEADS + col:2 * GDN_HEADS + col + 1]
            decay = jnp.exp(jnp.where(incl, gc - gr, -jnp.inf))
            kb = k * beta
            kb16, k16 = kb.astype(BF16), k.astype(BF16)
            m = jnp.where(strict, _dot_nt(kb16, k16) * decay, 0.0)
            t_inv = _unit_tri_inverse(m, ri, ci, eye)
            eg = jnp.exp(gc)
            rhs = jnp.concatenate([v * beta, kb * eg], axis=1)
            sol = _dot_f32(t_inv, rhs)
            u_, w_ = sol[:, :hd], sol[:, hd:]
            attn = _dot_nt(q.astype(BF16), k16) * decay
            qd = q * eg
            ks = k * jnp.exp(gl - gc)
            state = s_ref[h]
            wq = _dot(jnp.concatenate([w_, qd], axis=0).astype(BF16), state.astype(BF16))
            v_new = u_ - wq[:c_]
            vn16 = v_new.astype(BF16)
            o = wq[c_:] + _dot(attn.astype(BF16), vn16)
            s_ref[h] = state * jnp.exp(gl) + _dot_tn(ks.astype(BF16), vn16)
            cols = slice(h * hd, (h + 1) * hd)
            if final:
                o = o + of_ref[0, r0:r0 + c_, cols]
                z = main_ref[0, r0:r0 + c_, qkv_w + h * hd:qkv_w + (h + 1) * hd]
                y = o * lax.rsqrt(jnp.mean(o * o, axis=-1, keepdims=True) + NORM_EPS) * gn_ref[...]
                o_ref[0, r0:r0 + c_, cols] = y * _silu(z)
            else:
                o_ref[0, r0:r0 + c_, cols] = o
    sfin_ref[0] = s_ref[...]


def _gdn(gdn_in, gab, s0, prm, o_fwd, gn, *, tt, direction):
    b_, t, width = gdn_in.shape
    final = o_fwd is not None
    reverse = direction == 1
    main, halo = _halo_specs(tt, t, width, reverse)
    st_spec = pl.BlockSpec((1, GDN_HEADS, GDN_HEAD_DIM, GDN_HEAD_DIM), lambda b, i: (b, 0, 0, 0))
    out_tile = pl.BlockSpec((1, tt, GDN_DIM), main.index_map)
    in_specs = [main, halo, pl.BlockSpec((1, tt, 128), main.index_map), st_spec] + [_resident(p.shape) for p in prm]
    args = [gdn_in, gdn_in, gab, s0] + list(prm)
    if final:
        in_specs += [out_tile, _resident(gn.shape)]
        args += [o_fwd, gn]
    return pl.pallas_call(
        functools.partial(_gdn_kernel, tt=tt, direction=direction, final=final),
        out_shape=[jax.ShapeDtypeStruct((b_, t, GDN_DIM), F32),
                   jax.ShapeDtypeStruct((b_, GDN_HEADS, GDN_HEAD_DIM, GDN_HEAD_DIM), F32)],
        grid=(b_, t // tt),
        in_specs=in_specs,
        out_specs=[out_tile, st_spec],
        scratch_shapes=[pltpu.VMEM((tt + SUBLANES, 3 * GDN_DIM), F32),
                        pltpu.VMEM((GDN_HEADS, GDN_HEAD_DIM, GDN_HEAD_DIM), F32)],
        compiler_params=_cparams("parallel", "arbitrary"),
        name="gdn_rev" if reverse else "gdn_fwd",
    )(*args)


def _att_kernel(*refs, n_q, window):
    if window:
        q_ref, kp_ref, kc_ref, kn_ref, vp_ref, vc_ref, vn_ref, kx_ref, vx_ref, sink_ref, o_ref = refs
    else:
        q_ref, kx_ref, vx_ref, sink_ref, o_ref = refs
    i = pl.program_id(1)
    blk = ATT_BLOCK
    lane = lax.broadcasted_iota(jnp.int32, (blk, 128), 1)
    lo = lane < ATT_HEAD_DIM
    n_ctx = kx_ref.shape[1]
    if window:
        r = lax.broadcasted_iota(jnp.int32, (blk, 3 * blk + n_ctx), 0)
        c = lax.broadcasted_iota(jnp.int32, (blk, 3 * blk + n_ctx), 1)
        no_prev = jnp.where(i > 0, 0, blk)
        no_next = jnp.where(i < n_q - 1, 0, blk)
        mask = ((c >= blk) | (c >= r + no_prev)) & ((c < 2 * blk) | (c >= 3 * blk) | (c - 2 * blk <= r - no_next))
    group = ATT_HEADS // ATT_KV_HEADS
    for j in range(ATT_KV_HEADS):
        cols = slice(128 * j, 128 * (j + 1))
        if window:
            k_all = jnp.concatenate([kp_ref[0, :, cols], kc_ref[0, :, cols], kn_ref[0, :, cols], kx_ref[0, :, cols]], axis=0)
            v_all = jnp.concatenate([vp_ref[0, :, cols], vc_ref[0, :, cols], vn_ref[0, :, cols], vx_ref[0, :, cols]], axis=0)
        else:
            k_all, v_all = kx_ref[0, :, cols], vx_ref[0, :, cols]
        k_all, v_all = k_all.astype(BF16), v_all.astype(BF16)
        for pair in range(group // 2):
            slab = j * (group // 2) + pair
            qp = q_ref[0, :, 128 * slab:128 * (slab + 1)]
            outs = []
            for e in range(2):
                head = 2 * slab + e
                qh = jnp.where(lo if e == 0 else ~lo, qp, 0.0).astype(BF16)
                s = _dot_nt(qh, k_all)
                if window:
                    s = jnp.where(mask, s, NEG_BIG)
                sk = sink_ref[head]
                mx = jnp.maximum(jnp.max(s, axis=-1, keepdims=True), sk)
                p = jnp.exp(s - mx)
                den = jnp.sum(p, axis=-1, keepdims=True) + jnp.exp(sk - mx)
                outs.append(_dot(p.astype(BF16), v_all) / den)
            o_ref[0, :, 128 * slab:128 * (slab + 1)] = jnp.where(lo, outs[0], outs[1])


def _attention(q, k, v, kx, vx, sink, *, window):
    b_, t, _ = q.shape
    n_q = t // ATT_BLOCK
    n_ctx = kx.shape[1]
    cur = lambda b, i: (b, i, 0)
    prev = lambda b, i: (b, jnp.maximum(i - 1, 0), 0)
    nxt = lambda b, i: (b, jnp.minimum(i + 1, n_q - 1), 0)
    kv_w = kx.shape[-1]
    ctx_spec = pl.BlockSpec((1, n_ctx, kv_w), lambda b, i: (b, 0, 0))
    in_specs = [pl.BlockSpec((1, ATT_BLOCK, ATT_DIM), cur)]
    args = [q]
    if window:
        in_specs += [pl.BlockSpec((1, ATT_BLOCK, kv_w), f) for f in (prev, cur, nxt)] * 2
        args += [k, k, k, v, v, v]
    in_specs += [ctx_spec, ctx_spec, pl.BlockSpec(memory_space=pltpu.SMEM)]
    args += [kx, vx, sink]
    return pl.pallas_call(
        functools.partial(_att_kernel, n_q=n_q, window=window),
        out_shape=jax.ShapeDtypeStruct((b_, t, ATT_DIM), F32),
        grid=(b_, n_q),
        in_specs=in_specs,
        out_specs=pl.BlockSpec((1, ATT_BLOCK, ATT_DIM), cur),
        compiler_params=_cparams("parallel", "parallel"),
        name="window_attn" if window else "ctx_attn",
    )(*args)


def _merge_kernel(x_ref, mod_ref, yc_ref, yl_ref, yg_ref, ya_ref, wgate_ref, pc_ref, pl_ref, pg_ref, pa_ref, wout_ref,
                  g_ref, b_ref, o_ref):
    x = x_ref[0]
    m = mod_ref[0]
    h = (_ln(x) * (1.0 + m[4:5]) + m[3:4]).astype(BF16)
    mrg = None
    for n, (y_ref, p_ref) in enumerate(((yc_ref, pc_ref), (yl_ref, pl_ref), (yg_ref, pg_ref), (ya_ref, pa_ref))):
        gate = jax.nn.sigmoid(_dot(h, wgate_ref[:, n * D_MODEL:(n + 1) * D_MODEL]))
        term = gate * _dot(y_ref[0].astype(BF16), p_ref[...])
        mrg = term if mrg is None else mrg + term
    out = _dot(mrg.astype(BF16), wout_ref[...])
    s = DEEPNORM_ALPHA * x + m[5:6] * out
    o_ref[0] = _ln(s) * g_ref[...] + b_ref[...]


def _merge(x, mod3, ys, wgate, projs, wout, ng, nb, *, ctx, tm):
    b_, t, d = x.shape
    weights = [wgate, *projs, wout, ng, nb]
    return pl.pallas_call(
        _merge_kernel,
        out_shape=jax.ShapeDtypeStruct(x.shape, F32),
        grid=(b_, t // tm),
        in_specs=[_tok_spec(tm, d), _mod_spec(ctx)] + [_tok_spec(tm, y.shape[-1]) for y in ys]
                 + [_resident(w.shape) for w in weights],
        out_specs=_tok_spec(tm, d),
        compiler_params=_cparams("parallel", "parallel"),
        name="merge_ctx" if ctx else "merge",
    )(x, mod3, *ys, *weights)


def _block_diag_halves(w):
    n, bd, _ = w.shape
    per = n // 2
    out = jnp.zeros((2, per * bd, per * bd), F32)
    for blk in range(n):
        hh, s = divmod(blk, per)
        out = out.at[hh, s * bd:(s + 1) * bd, s * bd:(s + 1) * bd].set(w[blk])
    return out.astype(BF16)


def _rope_tables(t):
    pos = jnp.arange(t, dtype=jnp.int32)
    row = (pos // GRID_W).astype(F32)
    col = (pos % GRID_W).astype(F32)
    n_freq = ATT_HEAD_DIM // 4
    inv = ROPE_BASE ** (-jnp.arange(n_freq, dtype=F32) / n_freq)
    ang = jnp.concatenate([row[:, None] * inv, col[:, None] * inv], axis=-1)
    cos, sin = jnp.cos(ang), jnp.sin(ang)
    cos128 = jnp.tile(cos, (1, 4))
    sin128 = jnp.tile(jnp.concatenate([-sin, sin], axis=-1), (1, 2))
    return cos128, sin128


def _layer_params(l, ffn_w_in, ffn_w_out, w_in, lru_w_r, lru_w_i, proj_conv, proj_lru, proj_gdn, proj_att, w_out):
    fc = D_FF // FF_CHUNKS
    ffn = []
    for f in range(2):
        wgu = ffn_w_in[l, f].astype(BF16)
        wg = wgu[:, :D_FF].reshape(D_MODEL, FF_CHUNKS, fc).transpose(1, 0, 2)
        wu = wgu[:, D_FF:].reshape(D_MODEL, FF_CHUNKS, fc).transpose(1, 0, 2)
        wd = ffn_w_out[l, f].astype(BF16).reshape(FF_CHUNKS, fc, D_MODEL)
        ffn.append((wg, wu, wd))
    w = w_in[l]
    kd = ATT_HEAD_DIM
    wk = w[:, _OFF[9]:_OFF[10]]
    wv = w[:, _OFF[10]:_OFF[11]]
    dup = lambda m: jnp.concatenate([m[:, :kd], m[:, :kd], m[:, kd:], m[:, kd:]], axis=1)
    gab = jnp.pad(w[:, _OFF[6]:_OFF[8]], ((0, 0), (0, 128 - 4 * GDN_HEADS)))
    w_cat = jnp.concatenate([w[:, _OFF[0]:_OFF[2]], w[:, _OFF[2]:_OFF[4]], w[:, _OFF[4]:_OFF[6]], gab,
                             w[:, _OFF[8]:_OFF[9]], dup(wk), dup(wv)], axis=1).astype(BF16)
    wgate = w[:, _OFF[11]:_OFF[12]].astype(BF16)
    projs = [p[l].astype(BF16) for p in (proj_conv, proj_lru, proj_gdn, proj_att)]
    lru_w = [(_block_diag_halves(lru_w_r[l, d]), _block_diag_halves(lru_w_i[l, d])) for d in range(2)]
    return dict(ffn=ffn, w_cat=w_cat, wgate=wgate, projs=projs, wout=w_out[l].astype(BF16), lru_w=lru_w)


def kernel(x, c, ctx, c_ctx, ada_w, ada_b, norm_g, norm_b, ffn_w_in, ffn_w_out, w_in, conv_w, conv_b, conv_norm_g, conv_norm_b, lru_conv_w, lru_conv_b, lru_w_r, lru_b_r, lru_w_i, lru_b_i, lru_lam, gdn_conv_w, gdn_a_log, gdn_dt_bias, gdn_norm_g, att_sink, proj_conv, proj_lru, proj_gdn, proj_att, w_out):
    b_, t, d = x.shape
    lc = ctx.shape[1]
    assert d == D_MODEL and t % 512 == 0 and lc == 256 and b_ <= CTX_MOD_ROW
    c_all = jnp.zeros((MOD_ROWS, d), F32).at[:b_].set(c).at[CTX_MOD_ROW].set(c_ctx)
    mod_all = _modulation(c_all, ada_w, ada_b)
    rope_tabs = _rope_tables(t)
    row2 = lambda v: v.reshape(1, -1)
    xc = ctx
    for l in range(DEPTH):
        last = l == DEPTH - 1
        p = _layer_params(l, ffn_w_in, ffn_w_out, w_in, lru_w_r, lru_w_i, proj_conv, proj_lru, proj_gdn, proj_att, w_out)
        mod3 = mod_all[l]
        ng = [row2(norm_g[l, j]) for j in range(3)]
        nb = [row2(norm_b[l, j]) for j in range(3)]

        x = _ffn(x, mod3, *p["ffn"][0], ng[0], nb[0], j=0, ctx=False, tm=512)
        xc = _ffn(xc, mod3, *p["ffn"][0], ng[0], nb[0], j=0, ctx=True, tm=lc)

        z = _inproj(x, mod3, p["w_cat"], rope_tabs, ctx=False, tm=256)
        zc = _inproj(xc, mod3, p["w_cat"], None, ctx=True, tm=lc)
        conv_in, lru_in, gdn_in, gab, q, k, v = z
        conv_c, lru_c, gdn_c, gab_c, q_c, k_c, v_c = zc

        conv_prm = (conv_w[l], row2(conv_b[l]), row2(conv_norm_g[l]), row2(conv_norm_b[l]))
        y_conv = _conformer(conv_in, *conv_prm, tt=256)

        lru_prm = [(lru_conv_w[l, dd], row2(lru_conv_b[l, dd]), p["lru_w"][dd][0], row2(lru_b_r[l, dd]),
                    p["lru_w"][dd][1], row2(lru_b_i[l, dd]), row2(lru_lam[l, dd])) for dd in range(2)]
        h0 = jnp.zeros((b_, 1, LRU_DIM), F32)
        hc_f, st_f = _lru(lru_c, h0, lru_prm[0], None, tt=lc, reverse=False)
        hl_f, _ = _lru(lru_in, st_f, lru_prm[0], None, tt=256, reverse=False)
        y_lru_c, st_r = _lru(lru_c, h0, lru_prm[1], None if last else hc_f, tt=lc, reverse=True)
        y_lru, _ = _lru(lru_in, st_r, lru_prm[1], hl_f, tt=256, reverse=True)

        pad8 = lambda a: jnp.pad(a.reshape(1, -1), ((0, 0), (0, 128 - 2 * GDN_HEADS)))
        alog_row, dtb_row = pad8(gdn_a_log[l]), pad8(gdn_dt_bias[l])
        gdn_prm = [(gdn_conv_w[l, dd], alog_row, dtb_row) for dd in range(2)]
        gn = row2(gdn_norm_g[l])
        s0 = jnp.zeros((b_, GDN_HEADS, GDN_HEAD_DIM, GDN_HEAD_DIM), F32)
        oc_f, sg_f = _gdn(gdn_c, gab_c, s0, gdn_prm[0], None, None, tt=lc, direction=0)
        ol_f, _ = _gdn(gdn_in, gab, sg_f, gdn_prm[0], None, None, tt=256, direction=0)
        y_gdn_c, sg_r = _gdn(gdn_c, gab_c, s0, gdn_prm[1], None if last else oc_f, gn, tt=lc, direction=1)
        y_gdn, _ = _gdn(gdn_in, gab, sg_r, gdn_prm[1], ol_f, gn, tt=256, direction=1)

        y_att = _attention(q, k, v, k_c, v_c, att_sink[l], window=True)

        x = _merge(x, mod3, (y_conv, y_lru, y_gdn, y_att), p["wgate"], p["projs"], p["wout"], ng[1], nb[1],
                   ctx=False, tm=256)
        x = _ffn(x, mod3, *p["ffn"][1], ng[2], nb[2], j=2, ctx=False, tm=512)
        if not last:
            y_conv_c = _conformer(conv_c, *conv_prm, tt=lc)
            y_att_c = _attention(q_c, None, None, k_c, v_c, att_sink[l], window=False)
            xc = _merge(xc, mod3, (y_conv_c, y_lru_c, y_gdn_c, y_att_c), p["wgate"], p["projs"], p["wout"],
                        ng[1], nb[1], ctx=True, tm=lc)
            xc = _ffn(xc, mod3, *p["ffn"][1], ng[2], nb[2], j=2, ctx=True, tm=lc)
    return x
```

```python
import functools

import numpy as np
import jax
import jax.numpy as jnp
from jax import lax
from jax.experimental import pallas as pl
from jax.experimental.pallas import tpu as pltpu

F32 = jnp.float32
BF16 = jnp.bfloat16

D_MODEL = 1024
DEPTH = 2
GRID_W = 64
CONV_DIM = 512
CONV_WIDTH = 31
LRU_DIM = 512
LRU_BLOCKS = 8
LRU_CONV = 4
LRU_C = 8.0
GDN_HEADS = 4
GDN_HEAD_DIM = 128
GDN_DIM = GDN_HEADS * GDN_HEAD_DIM
GDN_CONV = 4
GDN_CHUNK = 64
ATT_HEADS = 8
ATT_KV_HEADS = 2
ATT_HEAD_DIM = 64
ATT_DIM = ATT_HEADS * ATT_HEAD_DIM
ATT_BLOCK = 128
ROPE_BASE = 10000.0
N_BRANCH = 4
D_FF = 2816
LN_EPS = 1e-5
NORM_EPS = 1e-6
DEEPNORM_ALPHA = (2 * DEPTH) ** 0.25

CTX_MOD_ROW = 4
MOD_ROWS = 8
FF_CHUNKS = 2
SUBLANES = 8
CONV_HALO = 16
VMEM_LIMIT = 56 * 1024 * 1024
NEG_BIG = -1e30

_OFF = np.cumsum([0, CONV_DIM, CONV_DIM, LRU_DIM, LRU_DIM, 3 * GDN_DIM, GDN_DIM, 2 * GDN_HEADS, 2 * GDN_HEADS,
                  ATT_DIM, ATT_KV_HEADS * ATT_HEAD_DIM, ATT_KV_HEADS * ATT_HEAD_DIM, N_BRANCH * D_MODEL]).tolist()


def _cparams(*sem):
    return pltpu.CompilerParams(dimension_semantics=sem, vmem_limit_bytes=VMEM_LIMIT)


def _resident(shape):
    nd = len(shape)
    return pl.BlockSpec(tuple(shape), lambda *_: (0,) * nd)


def _ln(x):
    mu = jnp.mean(x, axis=-1, keepdims=True)
    xc = x - mu
    var = jnp.mean(xc * xc, axis=-1, keepdims=True)
    return xc * lax.rsqrt(var + LN_EPS)


def _silu(x):
    return x * jax.nn.sigmoid(x)


def _softplus(x):
    return jnp.maximum(x, 0.0) + jnp.log1p(jnp.exp(-jnp.abs(x)))


def _dot(a, b):
    return jnp.dot(a, b, preferred_element_type=F32)


def _dot_nt(a, b):
    return lax.dot_general(a, b, (((1,), (1,)), ((), ())), preferred_element_type=F32)


def _dot_tn(a, b):
    return lax.dot_general(a, b, (((0,), (0,)), ((), ())), preferred_element_type=F32)


def _split(a):
    hi = a.astype(BF16)
    return hi, (a - hi.astype(F32)).astype(BF16)


def _mod_spec(ctx):
    if ctx:
        return pl.BlockSpec((1, 9, D_MODEL), lambda b, i: (CTX_MOD_ROW, 0, 0))
    return pl.BlockSpec((1, 9, D_MODEL), lambda b, i: (b, 0, 0))


def _tok_spec(tm, width):
    return pl.BlockSpec((1, tm, width), lambda b, i: (b, i, 0))


def _mod_kernel(c_ref, w_ref, b_ref, o_ref):
    s = _silu(c_ref[...]).astype(BF16)
    o_ref[0] = _dot(s, w_ref[0].astype(BF16)) + b_ref[0]


def _modulation(c_all, ada_w, ada_b):
    n_l, d, n = ada_w.shape
    tn = 1024
    out = pl.pallas_call(
        _mod_kernel,
        out_shape=jax.ShapeDtypeStruct((n_l, MOD_ROWS, n), F32),
        grid=(n_l, n // tn),
        in_specs=[pl.BlockSpec((MOD_ROWS, d), lambda l, j: (0, 0)),
                  pl.BlockSpec((1, d, tn), lambda l, j: (l, 0, j)),
                  pl.BlockSpec((1, 1, tn), lambda l, j: (l, 0, j))],
        out_specs=pl.BlockSpec((1, MOD_ROWS, tn), lambda l, j: (l, 0, j)),
        compiler_params=_cparams("arbitrary", "arbitrary"),
        name="adaln_mod",
    )(c_all, ada_w, ada_b.reshape(n_l, 1, n))
    return out.reshape(n_l, MOD_ROWS, 9, d)


def _ffn_kernel(x_ref, mod_ref, wg_ref, wu_ref, wd_ref, g_ref, b_ref, o_ref, *, j):
    x = x_ref[0]
    m = mod_ref[0]
    shift, scale, gate = m[3 * j:3 * j + 1], m[3 * j + 1:3 * j + 2], m[3 * j + 2:3 * j + 3]
    h = (_ln(x) * (1.0 + scale) + shift).astype(BF16)
    acc = None
    for f in range(FF_CHUNKS):
        g = _dot(h, wg_ref[f])
        u = _dot(h, wu_ref[f])
        a = (_silu(g) * u).astype(BF16)
        y = _dot(a, wd_ref[f])
        acc = y if acc is None else acc + y
    s = DEEPNORM_ALPHA * x + (0.5 * gate) * acc
    o_ref[0] = _ln(s) * g_ref[...] + b_ref[...]


def _ffn(x, mod3, wg, wu, wd, ng, nb, *, j, ctx, tm):
    b_, t, d = x.shape
    return pl.pallas_call(
        functools.partial(_ffn_kernel, j=j),
        out_shape=jax.ShapeDtypeStruct(x.shape, F32),
        grid=(b_, t // tm),
        in_specs=[_tok_spec(tm, d), _mod_spec(ctx), _resident(wg.shape), _resident(wu.shape), _resident(wd.shape),
                  _resident(ng.shape), _resident(nb.shape)],
        out_specs=_tok_spec(tm, d),
        compiler_params=_cparams("parallel", "parallel"),
        name="ffn_ctx" if ctx else "ffn",
    )(x, mod3, wg, wu, wd, ng, nb)


def _rope128(t, cos, sin_signed, lane_lo):
    swapped = jnp.where(lane_lo, pltpu.roll(t, 96, 1), pltpu.roll(t, 32, 1))
    return t * cos + swapped * sin_signed


def _inproj_kernel(*refs, rope):
    if rope:
        x_ref, mod_ref, w_ref, cos_ref, sin_ref, o_conv, o_lru, o_gdn, o_gab, o_q, o_k, o_v = refs
    else:
        x_ref, mod_ref, w_ref, o_conv, o_lru, o_gdn, o_gab, o_q, o_k, o_v = refs
    m = mod_ref[0]
    h = (_ln(x_ref[0]) * (1.0 + m[4:5]) + m[3:4]).astype(BF16)
    o_conv[0] = _dot(h, w_ref[:, 0:1024])
    o_lru[0] = _dot(h, w_ref[:, 1024:2048])
    o_gdn[0] = _dot(h, w_ref[:, 2048:4096])
    o_gab[0] = _dot(h, w_ref[:, 4096:4224])
    q = _dot(h, w_ref[:, 4224:4736])
    k = _dot(h, w_ref[:, 4736:4992])
    o_v[0] = _dot(h, w_ref[:, 4992:5248])
    scale = ATT_HEAD_DIM ** -0.5
    if rope:
        cos, sin = cos_ref[...], sin_ref[...]
        lane_lo = (lax.broadcasted_iota(jnp.int32, cos.shape, 1) % ATT_HEAD_DIM) < (ATT_HEAD_DIM // 2)
        for s in range(ATT_DIM // 128):
            o_q[0, :, 128 * s:128 * (s + 1)] = _rope128(q[:, 128 * s:128 * (s + 1)], cos, sin, lane_lo) * scale
        for s in range(2):
            o_k[0, :, 128 * s:128 * (s + 1)] = _rope128(k[:, 128 * s:128 * (s + 1)], cos, sin, lane_lo)
    else:
        o_q[0] = q * scale
        o_k[0] = k


_INPROJ_WIDTHS = (1024, 1024, 2048, 128, 512, 256, 256)


def _inproj(x, mod3, w_cat, rope_tabs, *, ctx, tm):
    b_, t, d = x.shape
    in_specs = [_tok_spec(tm, d), _mod_spec(ctx), _resident(w_cat.shape)]
    args = [x, mod3, w_cat]
    if rope_tabs is not None:
        in_specs += [pl.BlockSpec((tm, 128), lambda b, i: (i, 0))] * 2
        args += list(rope_tabs)
    return pl.pallas_call(
        functools.partial(_inproj_kernel, rope=rope_tabs is not None),
        out_shape=[jax.ShapeDtypeStruct((b_, t, w), F32) for w in _INPROJ_WIDTHS],
        grid=(b_, t // tm),
        in_specs=in_specs,
        out_specs=[_tok_spec(tm, w) for w in _INPROJ_WIDTHS],
        compiler_params=_cparams("parallel", "parallel"),
        name="inproj_ctx" if ctx else "inproj",
    )(*args)


def _conv_kernel(main_ref, prev_ref, next_ref, w_ref, cb_ref, g_ref, b_ref, o_ref, ubuf, *, tt, n_t, rb):
    i = pl.program_id(1)

    def glu(blk):
        return blk[:, :CONV_DIM] * jax.nn.sigmoid(blk[:, CONV_DIM:])

    n_rows = tt + 2 * CONV_HALO
    ubuf[0, 0:CONV_HALO] = jnp.where(i > 0, glu(prev_ref[0]), 0.0)
    ubuf[0, CONV_HALO:CONV_HALO + tt] = glu(main_ref[0])
    ubuf[0, CONV_HALO + tt:] = jnp.where(i < n_t - 1, glu(next_ref[0]), 0.0)
    for s in range(1, SUBLANES):
        ubuf[s, 0:n_rows - SUBLANES] = ubuf[0, s:s + n_rows - SUBLANES]
    base = CONV_HALO - (CONV_WIDTH - 1) // 2
    for r0 in range(0, tt, rb):
        acc = None
        for k in range(CONV_WIDTH):
            tile, s = divmod(base + k, SUBLANES)
            start = r0 + tile * SUBLANES
            term = ubuf[s, start:start + rb, :] * w_ref[k]
            acc = term if acc is None else acc + term
        y = _ln(acc + cb_ref[...]) * g_ref[...] + b_ref[...]
        o_ref[0, r0:r0 + rb, :] = _silu(y)


def _conformer(conv_in, w, cb, ng, nb, *, tt):
    b_, t, width = conv_in.shape
    n_t = t // tt
    hb = tt // CONV_HALO
    n_h = t // CONV_HALO
    rb = 2 * SUBLANES
    w = jnp.broadcast_to(w[:, None, :], (CONV_WIDTH, rb, CONV_DIM))
    return pl.pallas_call(
        functools.partial(_conv_kernel, tt=tt, n_t=n_t, rb=rb),
        out_shape=jax.ShapeDtypeStruct((b_, t, CONV_DIM), F32),
        grid=(b_, n_t),
        in_specs=[_tok_spec(tt, width),
                  pl.BlockSpec((1, CONV_HALO, width), lambda b, i: (b, jnp.maximum(i * hb - 1, 0), 0)),
                  pl.BlockSpec((1, CONV_HALO, width), lambda b, i: (b, jnp.minimum((i + 1) * hb, n_h - 1), 0)),
                  _resident(w.shape), _resident(cb.shape), _resident(ng.shape), _resident(nb.shape)],
        out_specs=_tok_spec(tt, CONV_DIM),
        scratch_shapes=[pltpu.VMEM((SUBLANES, tt + 2 * CONV_HALO, CONV_DIM), F32)],
        compiler_params=_cparams("parallel", "parallel"),
        name="conformer_conv",
    )(conv_in, conv_in, conv_in, w, cb, ng, nb)


def _halo_specs(tt, t, width, reverse):
    n_t = t // tt
    hb = tt // SUBLANES
    n_h = t // SUBLANES
    if reverse:
        main = pl.BlockSpec((1, tt, width), lambda b, i: (b, n_t - 1 - i, 0))
        halo = pl.BlockSpec((1, SUBLANES, width), lambda b, i: (b, jnp.minimum((n_t - i) * hb, n_h - 1), 0))
    else:
        main = pl.BlockSpec((1, tt, width), lambda b, i: (b, i, 0))
        halo = pl.BlockSpec((1, SUBLANES, width), lambda b, i: (b, jnp.maximum(i * hb - 1, 0), 0))
    return main, halo


def _fill_conv_buffer(xbuf, main, halo, first, tt, reverse):
    halo = jnp.where(first, 0.0, halo)
    if reverse:
        xbuf[0:tt] = main
        xbuf[tt:] = halo
        return lambda k: LRU_CONV - 1 - k
    xbuf[0:SUBLANES] = halo
    xbuf[SUBLANES:] = main
    return lambda k: SUBLANES - (LRU_CONV - 1) + k


def _lru_kernel(*refs, tt, reverse, final):
    if final:
        (main_ref, halo_ref, h0_ref, cw_ref, cb_ref, wr_ref, br_ref, wi_ref, bi_ref, lam_ref, hf_ref,
         o_ref, st_ref, xbuf, abuf, bbuf, carry) = refs
    else:
        (main_ref, halo_ref, h0_ref, cw_ref, cb_ref, wr_ref, br_ref, wi_ref, bi_ref, lam_ref,
         o_ref, st_ref, xbuf, abuf, bbuf, carry) = refs
    i = pl.program_id(1)

    @pl.when(i == 0)
    def _():
        carry[...] = h0_ref[0]

    off = _fill_conv_buffer(xbuf, main_ref[0, :, 0:LRU_DIM], halo_ref[0, :, 0:LRU_DIM], i == 0, tt, reverse)
    xc = cb_ref[...]
    for k in range(LRU_CONV):
        xc = xc + xbuf[off(k):off(k) + tt, :] * cw_ref[k:k + 1, :]
    xh = xc.astype(BF16)
    half = LRU_DIM // 2
    r_lin = jnp.concatenate([_dot(xh[:, :half], wr_ref[0]), _dot(xh[:, half:], wr_ref[1])], axis=1)
    i_lin = jnp.concatenate([_dot(xh[:, :half], wi_ref[0]), _dot(xh[:, half:], wi_ref[1])], axis=1)
    r = jax.nn.sigmoid(r_lin + br_ref[...])
    gi = jax.nn.sigmoid(i_lin + bi_ref[...])
    log_a = (-LRU_C) * r * _softplus(-lam_ref[...])
    a = jnp.exp(log_a)
    th = jnp.tanh(log_a)
    bb = jnp.sqrt(-2.0 * th / (1.0 - th)) * (gi * xc)

    n_g = tt // SUBLANES
    a3 = a.reshape(n_g, SUBLANES, LRU_DIM)
    b3 = bb.reshape(n_g, SUBLANES, LRU_DIM)
    row = lax.broadcasted_iota(jnp.int32, a3.shape, 1)
    for s in (1, 2, 4):
        if reverse:
            keep = row < SUBLANES - s
            a_sh = pltpu.roll(a3, SUBLANES - s, 1)
            b_sh = pltpu.roll(b3, SUBLANES - s, 1)
        else:
            keep = row >= s
            a_sh = pltpu.roll(a3, s, 1)
            b_sh = pltpu.roll(b3, s, 1)
        b3 = jnp.where(keep, a3 * b_sh + b3, b3)
        a3 = jnp.where(keep, a3 * a_sh, a3)
    abuf[...] = a3.reshape(tt, LRU_DIM)
    bbuf[...] = b3.reshape(tt, LRU_DIM)

    hprev = carry[...]
    order = range(n_g - 1, -1, -1) if reverse else range(n_g)
    edge = 0 if reverse else SUBLANES - 1
    for g in order:
        rows = slice(g * SUBLANES, (g + 1) * SUBLANES)
        hg = abuf[rows, :] * hprev + bbuf[rows, :]
        hprev = hg[edge:edge + 1, :]
        if final:
            gate = jax.nn.gelu(main_ref[0, rows, LRU_DIM:2 * LRU_DIM])
            o_ref[0, rows, :] = (hf_ref[0, rows, :] + hg) * gate
        else:
            o_ref[0, rows, :] = hg
    carry[...] = hprev
    st_ref[0] = hprev


def _lru(lru_in, h0, prm, h_fwd, *, tt, reverse):
    b_, t, width = lru_in.shape
    final = h_fwd is not None
    main, halo = _halo_specs(tt, t, width, reverse)
    in_specs = [main, halo, pl.BlockSpec((1, 1, LRU_DIM), lambda b, i: (b, 0, 0))] + [_resident(p.shape) for p in prm]
    args = [lru_in, lru_in, h0] + list(prm)
    out_tile = pl.BlockSpec((1, tt, LRU_DIM), main.index_map)
    if final:
        in_specs.append(out_tile)
        args.append(h_fwd)
    return pl.pallas_call(
        functools.partial(_lru_kernel, tt=tt, reverse=reverse, final=final),
        out_shape=[jax.ShapeDtypeStruct((b_, t, LRU_DIM), F32), jax.ShapeDtypeStruct((b_, 1, LRU_DIM), F32)],
        grid=(b_, t // tt),
        in_specs=in_specs,
        out_specs=[out_tile, pl.BlockSpec((1, 1, LRU_DIM), lambda b, i: (b, 0, 0))],
        scratch_shapes=[pltpu.VMEM((tt + SUBLANES, LRU_DIM), F32), pltpu.VMEM((tt, LRU_DIM), F32),
                        pltpu.VMEM((tt, LRU_DIM), F32), pltpu.VMEM((1, LRU_DIM), F32)],
        compiler_params=_cparams("parallel", "arbitrary"),
        name="rglru_rev" if reverse else "rglru_fwd",
    )(*args)


def _unit_tri_inverses(ms, ri, ci, eye):
    def same_block(log2_size):
        return jnp.right_shift(ri, log2_size) == jnp.right_shift(ci, log2_size)

    base = same_block(3)
    ps = [jnp.where(base, -m, 0.0) for m in ms]
    xs = [eye + p for p in ps]
    for _ in range(2):
        p16 = [p.astype(BF16) for p in ps]
        ps = [_dot(p, p) for p in p16]
        xs = [x + _dot(x.astype(BF16), p.astype(BF16)) for x, p in zip(xs, ps)]
    for log2_size in (3, 4, 5):
        level = same_block(log2_size + 1) & ~same_block(log2_size)
        x16 = [x.astype(BF16) for x in xs]
        ys = [_dot(x, jnp.where(level, m, 0.0).astype(BF16)) for x, m in zip(x16, ms)]
        xs = [x - _dot(y.astype(BF16), xb) for x, y, xb in zip(xs, ys, x16)]
    return xs


def _gdn_kernel(*refs, tt, direction, final):
    if final:
        (main_ref, halo_ref, gab_ref, s0_ref, cw_ref, alog_ref, dtb_ref, of_ref, gn_ref,
         o_ref, sfin_ref, xbuf, s_ref) = refs
    else:
        (main_ref, halo_ref, gab_ref, s0_ref, cw_ref, alog_ref, dtb_ref,
         o_ref, sfin_ref, xbuf, s_ref) = refs
    reverse = direction == 1
    i = pl.program_id(1)
    c_ = GDN_CHUNK
    hd = GDN_HEAD_DIM
    qkv_w = 3 * GDN_DIM

    @pl.when(i == 0)
    def _():
        s_ref[...] = s0_ref[0]

    off = _fill_conv_buffer(xbuf, main_ref[0, :, 0:qkv_w], halo_ref[0, :, 0:qkv_w], i == 0, tt, reverse)

    ri = lax.broadcasted_iota(jnp.int32, (c_, c_), 0)
    ci = lax.broadcasted_iota(jnp.int32, (c_, c_), 1)
    incl = (ci >= ri) if reverse else (ci <= ri)
    strict = (ci > ri) if reverse else (ci < ri)
    eye = (ri == ci).astype(F32)
    ones_tri = jnp.where(incl, 1.0, 0.0).astype(BF16)
    last = 0 if reverse else c_ - 1

    n_c = tt // c_
    chunks = list(range(n_c - 1, -1, -1) if reverse else range(n_c))
    heads = range(GDN_HEADS)

    pre = {}
    for c in chunks:
        r0 = c * c_
        acc = None
        for k in range(GDN_CONV):
            term = xbuf[r0 + off(k):r0 + off(k) + c_, :] * cw_ref[k:k + 1, :]
            acc = term if acc is None else acc + term
        u = _silu(acc)
        ga = gab_ref[0, r0:r0 + c_, :]
        g_all = -jnp.exp(alog_ref[...]) * _softplus(ga + dtb_ref[...])
        beta_all = jax.nn.sigmoid(ga)
        g_hi, g_lo = _split(g_all)
        g_lo2 = (g_all - g_hi.astype(F32) - g_lo.astype(F32)).astype(BF16)
        gcum = _dot(ones_tri, g_hi) + (_dot(ones_tri, g_lo) + _dot(ones_tri, g_lo2))
        gcum_t = gcum.T
        g_last = gcum[last:last + 1, :]
        for h in heads:
            col = direction * GDN_HEADS + h
            q = u[:, h * hd:(h + 1) * hd]
            k = u[:, GDN_DIM + h * hd:GDN_DIM + (h + 1) * hd]
            v = u[:, 2 * GDN_DIM + h * hd:2 * GDN_DIM + (h + 1) * hd]
            q = q * (lax.rsqrt(jnp.sum(q * q, axis=-1, keepdims=True) + NORM_EPS) * (hd ** -0.5))
            k = k * lax.rsqrt(jnp.sum(k * k, axis=-1, keepdims=True) + NORM_EPS)
            gc = gcum[:, col:col + 1]
            gl = g_last[:, col:col + 1]
            beta = beta_all[:, 2 * GDN_HEADS + col:2 * GDN_HEADS + col + 1]
            decay = jnp.exp(jnp.where(incl, gc - gcum_t[col:col + 1, :], -jnp.inf))
            kb = k * beta
            k16 = k.astype(BF16)
            eg = jnp.exp(gc)
            pre[c, h] = dict(
                m=jnp.where(strict, _dot_nt(kb.astype(BF16), k16) * decay, 0.0),
                attn=(_dot_nt(q.astype(BF16), k16) * decay).astype(BF16),
                rhs=jnp.concatenate([v * beta, kb * eg], axis=1).astype(BF16),
                qd=q * eg, ks=(k * jnp.exp(gl - gc)).astype(BF16), cd=jnp.exp(gl))
    keys = [(c, h) for c in chunks for h in heads]
    t_inv = _unit_tri_inverses([pre[key]["m"] for key in keys], ri, ci, eye)
    for key, t in zip(keys, t_inv):
        sol = _dot(t.astype(BF16), pre[key]["rhs"])
        pre[key]["u"] = sol[:, :hd]
        pre[key]["wq"] = jnp.concatenate([sol[:, hd:], pre[key]["qd"]], axis=0).astype(BF16)

    state = [s_ref[h] for h in heads]
    for c in chunks:
        r0 = c * c_
        wq = [_dot(pre[c, h]["wq"], state[h].astype(BF16)) for h in heads]
        vn16 = [(pre[c, h]["u"] - wq[h][:c_]).astype(BF16) for h in heads]
        outs = [wq[h][c_:] + _dot(pre[c, h]["attn"], vn16[h]) for h in heads]
        state = [state[h] * pre[c, h]["cd"] + _dot_tn(pre[c, h]["ks"], vn16[h]) for h in heads]
        for h in heads:
            cols = slice(h * hd, (h + 1) * hd)
            o = outs[h]
            if final:
                o = o + of_ref[0, r0:r0 + c_, cols]
                z = main_ref[0, r0:r0 + c_, qkv_w + h * hd:qkv_w + (h + 1) * hd]
                y = o * lax.rsqrt(jnp.mean(o * o, axis=-1, keepdims=True) + NORM_EPS) * gn_ref[...]
                o_ref[0, r0:r0 + c_, cols] = y * _silu(z)
            else:
                o_ref[0, r0:r0 + c_, cols] = o
    for h in heads:
        s_ref[h] = state[h]
    sfin_ref[0] = s_ref[...]


def _gdn(gdn_in, gab, s0, prm, o_fwd, gn, *, tt, direction):
    b_, t, width = gdn_in.shape
    final = o_fwd is not None
    reverse = direction == 1
    main, halo = _halo_specs(tt, t, width, reverse)
    st_spec = pl.BlockSpec((1, GDN_HEADS, GDN_HEAD_DIM, GDN_HEAD_DIM), lambda b, i: (b, 0, 0, 0))
    out_tile = pl.BlockSpec((1, tt, GDN_DIM), main.index_map)
    in_specs = [main, halo, pl.BlockSpec((1, tt, 128), main.index_map), st_spec] + [_resident(p.shape) for p in prm]
    args = [gdn_in, gdn_in, gab, s0] + list(prm)
    if final:
        in_specs += [out_tile, _resident(gn.shape)]
        args += [o_fwd, gn]
    return pl.pallas_call(
        functools.partial(_gdn_kernel, tt=tt, direction=direction, final=final),
        out_shape=[jax.ShapeDtypeStruct((b_, t, GDN_DIM), F32),
                   jax.ShapeDtypeStruct((b_, GDN_HEADS, GDN_HEAD_DIM, GDN_HEAD_DIM), F32)],
        grid=(b_, t // tt),
        in_specs=in_specs,
        out_specs=[out_tile, st_spec],
        scratch_shapes=[pltpu.VMEM((tt + SUBLANES, 3 * GDN_DIM), F32),
                        pltpu.VMEM((GDN_HEADS, GDN_HEAD_DIM, GDN_HEAD_DIM), F32)],
        compiler_params=_cparams("parallel", "arbitrary"),
        name="gdn_rev" if reverse else "gdn_fwd",
    )(*args)


def _att_kernel(*refs, n_q, window):
    if window:
        q_ref, kp_ref, kc_ref, kn_ref, vp_ref, vc_ref, vn_ref, kx_ref, vx_ref, sink_ref, o_ref = refs
    else:
        q_ref, kx_ref, vx_ref, sink_ref, o_ref = refs
    i = pl.program_id(1)
    blk = ATT_BLOCK
    group = ATT_HEADS // ATT_KV_HEADS
    lane = lax.broadcasted_iota(jnp.int32, (blk, 128), 1)
    lo = lane < ATT_HEAD_DIM
    n_ctx = kx_ref.shape[1]
    if window:
        r = lax.broadcasted_iota(jnp.int32, (group * blk, 3 * blk + n_ctx), 0) & (blk - 1)
        c = lax.broadcasted_iota(jnp.int32, (group * blk, 3 * blk + n_ctx), 1)
        no_prev = jnp.where(i > 0, 0, blk)
        no_next = jnp.where(i < n_q - 1, 0, blk)
        mask = ((c >= blk) | (c >= r + no_prev)) & ((c < 2 * blk) | (c >= 3 * blk) | (c - 2 * blk <= r - no_next))
    row_head = lax.broadcasted_iota(jnp.int32, (group * blk, 1), 0) // blk
    for j in range(ATT_KV_HEADS):
        cols = slice(128 * j, 128 * (j + 1))
        if window:
            k_all = jnp.concatenate([kp_ref[0, :, cols], kc_ref[0, :, cols], kn_ref[0, :, cols], kx_ref[0, :, cols]], axis=0)
            v_all = jnp.concatenate([vp_ref[0, :, cols], vc_ref[0, :, cols], vn_ref[0, :, cols], vx_ref[0, :, cols]], axis=0)
        else:
            k_all, v_all = kx_ref[0, :, cols], vx_ref[0, :, cols]
        k_all, v_all = k_all.astype(BF16), v_all.astype(BF16)
        q_rows = []
        sk = jnp.zeros((group * blk, 1), F32)
        for e in range(group):
            head = group * j + e
            qp = q_ref[0, :, 128 * (head // 2):128 * (head // 2 + 1)]
            q_rows.append(jnp.where(lo if head % 2 == 0 else ~lo, qp, 0.0).astype(BF16))
            sk = jnp.where(row_head == e, sink_ref[head], sk)
        s = _dot_nt(jnp.concatenate(q_rows, axis=0), k_all)
        if window:
            s = jnp.where(mask, s, NEG_BIG)
        mx = jnp.maximum(jnp.max(s, axis=-1, keepdims=True), sk)
        p = jnp.exp(s - mx)
        den = jnp.sum(p, axis=-1, keepdims=True) + jnp.exp(sk - mx)
        o = _dot(p.astype(BF16), v_all) / den
        for pair in range(group // 2):
            slab = (group * j) // 2 + pair
            o_ref[0, :, 128 * slab:128 * (slab + 1)] = jnp.where(lo, o[2 * pair * blk:(2 * pair + 1) * blk],
                                                                 o[(2 * pair + 1) * blk:(2 * pair + 2) * blk])


def _attention(q, k, v, kx, vx, sink, *, window):
    b_, t, _ = q.shape
    n_q = t // ATT_BLOCK
    n_ctx = kx.shape[1]
    cur = lambda b, i: (b, i, 0)
    prev = lambda b, i: (b, jnp.maximum(i - 1, 0), 0)
    nxt = lambda b, i: (b, jnp.minimum(i + 1, n_q - 1), 0)
    kv_w = kx.shape[-1]
    ctx_spec = pl.BlockSpec((1, n_ctx, kv_w), lambda b, i: (b, 0, 0))
    in_specs = [pl.BlockSpec((1, ATT_BLOCK, ATT_DIM), cur)]
    args = [q]
    if window:
        in_specs += [pl.BlockSpec((1, ATT_BLOCK, kv_w), f) for f in (prev, cur, nxt)] * 2
        args += [k, k, k, v, v, v]
    in_specs += [ctx_spec, ctx_spec, pl.BlockSpec(memory_space=pltpu.SMEM)]
    args += [kx, vx, sink]
    return pl.pallas_call(
        functools.partial(_att_kernel, n_q=n_q, window=window),
        out_shape=jax.ShapeDtypeStruct((b_, t, ATT_DIM), F32),
        grid=(b_, n_q),
        in_specs=in_specs,
        out_specs=pl.BlockSpec((1, ATT_BLOCK, ATT_DIM), cur),
        compiler_params=_cparams("parallel", "parallel"),
        name="window_attn" if window else "ctx_attn",
    )(*args)


def _merge_kernel(x_ref, mod_ref, yc_ref, yl_ref, yg_ref, ya_ref, wgate_ref, pc_ref, pl_ref, pg_ref, pa_ref, wout_ref,
                  g_ref, b_ref, o_ref):
    x = x_ref[0]
    m = mod_ref[0]
    h = (_ln(x) * (1.0 + m[4:5]) + m[3:4]).astype(BF16)
    mrg = None
    for n, (y_ref, p_ref) in enumerate(((yc_ref, pc_ref), (yl_ref, pl_ref), (yg_ref, pg_ref), (ya_ref, pa_ref))):
        gate = jax.nn.sigmoid(_dot(h, wgate_ref[:, n * D_MODEL:(n + 1) * D_MODEL]))
        term = gate * _dot(y_ref[0].astype(BF16), p_ref[...])
        mrg = term if mrg is None else mrg + term
    out = _dot(mrg.astype(BF16), wout_ref[...])
    s = DEEPNORM_ALPHA * x + m[5:6] * out
    o_ref[0] = _ln(s) * g_ref[...] + b_ref[...]


def _merge(x, mod3, ys, wgate, projs, wout, ng, nb, *, ctx, tm):
    b_, t, d = x.shape
    weights = [wgate, *projs, wout, ng, nb]
    return pl.pallas_call(
        _merge_kernel,
        out_shape=jax.ShapeDtypeStruct(x.shape, F32),
        grid=(b_, t // tm),
        in_specs=[_tok_spec(tm, d), _mod_spec(ctx)] + [_tok_spec(tm, y.shape[-1]) for y in ys]
                 + [_resident(w.shape) for w in weights],
        out_specs=_tok_spec(tm, d),
        compiler_params=_cparams("parallel", "parallel"),
        name="merge_ctx" if ctx else "merge",
    )(x, mod3, *ys, *weights)


def _block_diag_halves(w):
    n, bd, _ = w.shape
    per = n // 2
    on_diag = jnp.eye(per, dtype=jnp.bool_)[None, :, None, :, None]
    out = jnp.where(on_diag, w.reshape(2, per, bd, 1, bd), 0.0)
    return out.reshape(2, per * bd, per * bd).astype(BF16)


def _rope_tables(t):
    pos = jnp.arange(t, dtype=jnp.int32)
    row = (pos // GRID_W).astype(F32)
    col = (pos % GRID_W).astype(F32)
    n_freq = ATT_HEAD_DIM // 4
    inv = ROPE_BASE ** (-jnp.arange(n_freq, dtype=F32) / n_freq)
    ang = jnp.concatenate([row[:, None] * inv, col[:, None] * inv], axis=-1)
    cos, sin = jnp.cos(ang), jnp.sin(ang)
    cos128 = jnp.tile(cos, (1, 4))
    sin128 = jnp.tile(jnp.concatenate([-sin, sin], axis=-1), (1, 2))
    return cos128, sin128


def _layer_params(l, ffn_w_in, ffn_w_out, w_in, lru_w_r, lru_w_i, proj_conv, proj_lru, proj_gdn, proj_att, w_out):
    fc = D_FF // FF_CHUNKS
    ffn = []
    for f in range(2):
        wgu = ffn_w_in[l, f].astype(BF16)
        wg = wgu[:, :D_FF].reshape(D_MODEL, FF_CHUNKS, fc).transpose(1, 0, 2)
        wu = wgu[:, D_FF:].reshape(D_MODEL, FF_CHUNKS, fc).transpose(1, 0, 2)
        wd = ffn_w_out[l, f].astype(BF16).reshape(FF_CHUNKS, fc, D_MODEL)
        ffn.append((wg, wu, wd))
    w = w_in[l]
    kd = ATT_HEAD_DIM
    wk = w[:, _OFF[9]:_OFF[10]]
    wv = w[:, _OFF[10]:_OFF[11]]
    dup = lambda m: jnp.concatenate([m[:, :kd], m[:, :kd], m[:, kd:], m[:, kd:]], axis=1)
    gab = jnp.pad(w[:, _OFF[6]:_OFF[8]], ((0, 0), (0, 128 - 4 * GDN_HEADS)))
    w_cat = jnp.concatenate([w[:, _OFF[0]:_OFF[2]], w[:, _OFF[2]:_OFF[4]], w[:, _OFF[4]:_OFF[6]], gab,
                             w[:, _OFF[8]:_OFF[9]], dup(wk), dup(wv)], axis=1).astype(BF16)
    wgate = w[:, _OFF[11]:_OFF[12]].astype(BF16)
    projs = [p[l].astype(BF16) for p in (proj_conv, proj_lru, proj_gdn, proj_att)]
    lru_w = [(_block_diag_halves(lru_w_r[l, d]), _block_diag_halves(lru_w_i[l, d])) for d in range(2)]
    return dict(ffn=ffn, w_cat=w_cat, wgate=wgate, projs=projs, wout=w_out[l].astype(BF16), lru_w=lru_w)


def kernel(x, c, ctx, c_ctx, ada_w, ada_b, norm_g, norm_b, ffn_w_in, ffn_w_out, w_in, conv_w, conv_b, conv_norm_g, conv_norm_b, lru_conv_w, lru_conv_b, lru_w_r, lru_b_r, lru_w_i, lru_b_i, lru_lam, gdn_conv_w, gdn_a_log, gdn_dt_bias, gdn_norm_g, att_sink, proj_conv, proj_lru, proj_gdn, proj_att, w_out):
    b_, t, d = x.shape
    lc = ctx.shape[1]
    assert d == D_MODEL and t % 512 == 0 and lc == 256 and b_ <= CTX_MOD_ROW
    c_all = jnp.zeros((MOD_ROWS, d), F32).at[:b_].set(c).at[CTX_MOD_ROW].set(c_ctx)
    mod_all = _modulation(c_all, ada_w, ada_b)
    rope_tabs = _rope_tables(t)
    row2 = lambda v: v.reshape(1, -1)
    xc = ctx
    for l in range(DEPTH):
        last = l == DEPTH - 1
        p = _layer_params(l, ffn_w_in, ffn_w_out, w_in, lru_w_r, lru_w_i, proj_conv, proj_lru, proj_gdn, proj_att, w_out)
        mod3 = mod_all[l]
        ng = [row2(norm_g[l, j]) for j in range(3)]
        nb = [row2(norm_b[l, j]) for j in range(3)]

        x = _ffn(x, mod3, *p["ffn"][0], ng[0], nb[0], j=0, ctx=False, tm=512)
        xc = _ffn(xc, mod3, *p["ffn"][0], ng[0], nb[0], j=0, ctx=True, tm=lc)

        z = _inproj(x, mod3, p["w_cat"], rope_tabs, ctx=False, tm=256)
        zc = _inproj(xc, mod3, p["w_cat"], None, ctx=True, tm=lc)
        conv_in, lru_in, gdn_in, gab, q, k, v = z
        conv_c, lru_c, gdn_c, gab_c, q_c, k_c, v_c = zc

        conv_prm = (conv_w[l], row2(conv_b[l]), row2(conv_norm_g[l]), row2(conv_norm_b[l]))
        y_conv = _conformer(conv_in, *conv_prm, tt=256)

        lru_prm = [(lru_conv_w[l, dd], row2(lru_conv_b[l, dd]), p["lru_w"][dd][0], row2(lru_b_r[l, dd]),
                    p["lru_w"][dd][1], row2(lru_b_i[l, dd]), row2(lru_lam[l, dd])) for dd in range(2)]
        h0 = jnp.zeros((b_, 1, LRU_DIM), F32)
        hc_f, st_f = _lru(lru_c, h0, lru_prm[0], None, tt=lc, reverse=False)
        hl_f, _ = _lru(lru_in, st_f, lru_prm[0], None, tt=256, reverse=False)
        y_lru_c, st_r = _lru(lru_c, h0, lru_prm[1], None if last else hc_f, tt=lc, reverse=True)
        y_lru, _ = _lru(lru_in, st_r, lru_prm[1], hl_f, tt=256, reverse=True)

        pad8 = lambda a: jnp.pad(a.reshape(1, -1), ((0, 0), (0, 128 - 2 * GDN_HEADS)))
        alog_row, dtb_row = pad8(gdn_a_log[l]), pad8(gdn_dt_bias[l])
        gdn_prm = [(gdn_conv_w[l, dd], alog_row, dtb_row) for dd in range(2)]
        gn = row2(gdn_norm_g[l])
        s0 = jnp.zeros((b_, GDN_HEADS, GDN_HEAD_DIM, GDN_HEAD_DIM), F32)
        oc_f, sg_f = _gdn(gdn_c, gab_c, s0, gdn_prm[0], None, None, tt=lc, direction=0)
        ol_f, _ = _gdn(gdn_in, gab, sg_f, gdn_prm[0], None, None, tt=256, direction=0)
        y_gdn_c, sg_r = _gdn(gdn_c, gab_c, s0, gdn_prm[1], None if last else oc_f, gn, tt=lc, direction=1)
        y_gdn, _ = _gdn(gdn_in, gab, sg_r, gdn_prm[1], ol_f, gn, tt=256, direction=1)

        y_att = _attention(q, k, v, k_c, v_c, att_sink[l], window=True)

        x = _merge(x, mod3, (y_conv, y_lru, y_gdn, y_att), p["wgate"], p["projs"], p["wout"], ng[1], nb[1],
                   ctx=False, tm=256)
        x = _ffn(x, mod3, *p["ffn"][1], ng[2], nb[2], j=2, ctx=False, tm=512)
        if not last:
            y_conv_c = _conformer(conv_c, *conv_prm, tt=lc)
            y_att_c = _attention(q_c, None, None, k_c, v_c, att_sink[l], window=False)
            xc = _merge(xc, mod3, (y_conv_c, y_lru_c, y_gdn_c, y_att_c), p["wgate"], p["projs"], p["wout"],
                        ng[1], nb[1], ctx=True, tm=lc)
            xc = _ffn(xc, mod3, *p["ffn"][1], ng[2], nb[2], j=2, ctx=True, tm=lc)
    return x
```

```python
import functools

import numpy as np
import jax
import jax.numpy as jnp
from jax import lax
from jax.experimental import pallas as pl
from jax.experimental.pallas import tpu as pltpu

F32 = jnp.float32
BF16 = jnp.bfloat16

D_MODEL = 1024
DEPTH = 2
GRID_W = 64
CONV_DIM = 512
CONV_WIDTH = 31
LRU_DIM = 512
LRU_BLOCKS = 8
LRU_CONV = 4
LRU_C = 8.0
GDN_HEADS = 4
GDN_HEAD_DIM = 128
GDN_DIM = GDN_HEADS * GDN_HEAD_DIM
GDN_CONV = 4
GDN_CHUNK = 64
ATT_HEADS = 8
ATT_KV_HEADS = 2
ATT_HEAD_DIM = 64
ATT_DIM = ATT_HEADS * ATT_HEAD_DIM
ATT_BLOCK = 128
ROPE_BASE = 10000.0
N_BRANCH = 4
D_FF = 2816
LN_EPS = 1e-5
NORM_EPS = 1e-6
DEEPNORM_ALPHA = (2 * DEPTH) ** 0.25

CTX_MOD_ROW = 4
MOD_ROWS = 8
FF_CHUNKS = 2
SUBLANES = 8
CONV_HALO = 16
VMEM_LIMIT = 56 * 1024 * 1024
FFN_GROUP = 512
FFN_TILE = 2 * FFN_GROUP
PROJ_TILE = 512
SEQ_TILE = 512
ATT_QUERY_BLOCKS = 2
NEG_BIG = -1e30

_OFF = np.cumsum([0, CONV_DIM, CONV_DIM, LRU_DIM, LRU_DIM, 3 * GDN_DIM, GDN_DIM, 2 * GDN_HEADS, 2 * GDN_HEADS,
                  ATT_DIM, ATT_KV_HEADS * ATT_HEAD_DIM, ATT_KV_HEADS * ATT_HEAD_DIM, N_BRANCH * D_MODEL]).tolist()


def _cparams(*sem):
    return pltpu.CompilerParams(dimension_semantics=sem, vmem_limit_bytes=VMEM_LIMIT)


def _resident(shape):
    nd = len(shape)
    return pl.BlockSpec(tuple(shape), lambda *_: (0,) * nd, pipeline_mode=pl.Buffered(1))


def _ln(x):
    mu = jnp.mean(x, axis=-1, keepdims=True)
    xc = x - mu
    var = jnp.mean(xc * xc, axis=-1, keepdims=True)
    return xc * lax.rsqrt(var + LN_EPS)


def _silu(x):
    return x * jax.nn.sigmoid(x)


def _softplus(x):
    return jnp.maximum(x, 0.0) + jnp.log1p(jnp.exp(-jnp.abs(x)))


def _dot(a, b):
    return jnp.dot(a, b, preferred_element_type=F32)


def _dot_nt(a, b):
    return lax.dot_general(a, b, (((1,), (1,)), ((), ())), preferred_element_type=F32)


def _dot_tn(a, b):
    return lax.dot_general(a, b, (((0,), (0,)), ((), ())), preferred_element_type=F32)


def _split(a):
    hi = a.astype(BF16)
    return hi, (a - hi.astype(F32)).astype(BF16)


def _mod_spec(ctx):
    if ctx:
        return pl.BlockSpec((1, 9, D_MODEL), lambda b, i: (CTX_MOD_ROW, 0, 0))
    return pl.BlockSpec((1, 9, D_MODEL), lambda b, i: (b, 0, 0))


def _tok_spec(tm, width):
    return pl.BlockSpec((1, tm, width), lambda b, i: (b, i, 0))


def _mod_kernel(c_ref, w_ref, b_ref, o_ref):
    s = _silu(c_ref[...]).astype(BF16)
    o_ref[0] = _dot(s, w_ref[0].astype(BF16)) + b_ref[0]


def _modulation(c_all, ada_w, ada_b):
    n_l, d, n = ada_w.shape
    tn = 1024
    out = pl.pallas_call(
        _mod_kernel,
        out_shape=jax.ShapeDtypeStruct((n_l, MOD_ROWS, n), F32),
        grid=(n_l, n // tn),
        in_specs=[pl.BlockSpec((MOD_ROWS, d), lambda l, j: (0, 0)),
                  pl.BlockSpec((1, d, tn), lambda l, j: (l, 0, j)),
                  pl.BlockSpec((1, 1, tn), lambda l, j: (l, 0, j))],
        out_specs=pl.BlockSpec((1, MOD_ROWS, tn), lambda l, j: (l, 0, j)),
        compiler_params=_cparams("arbitrary", "arbitrary"),
        name="adaln_mod",
    )(c_all, ada_w, ada_b.reshape(n_l, 1, n))
    return out.reshape(n_l, MOD_ROWS, 9, d)


def _ffn_kernel(x_ref, mod_ref, wg_ref, wu_ref, wd_ref, g_ref, b_ref, o_ref, *, j, n_sub):
    m = mod_ref[0]
    shift, scale, gate = m[3 * j:3 * j + 1], m[3 * j + 1:3 * j + 2], m[3 * j + 2:3 * j + 3]
    rows = x_ref.shape[1] // n_sub

    def prologue(r):
        return (_ln(x_ref[0, r * rows:(r + 1) * rows]) * (1.0 + scale) + shift).astype(BF16)

    def epilogue(r, acc):
        s = DEEPNORM_ALPHA * x_ref[0, r * rows:(r + 1) * rows] + (0.5 * gate) * acc
        o_ref[0, r * rows:(r + 1) * rows] = _ln(s) * g_ref[...] + b_ref[...]

    h = prologue(0)
    pending = None
    for r in range(n_sub):
        acc = None
        h_next = None
        for f in range(FF_CHUNKS):
            g = _dot(h, wg_ref[f])
            u = _dot(h, wu_ref[f])
            if f == 0:
                if pending is not None:
                    epilogue(*pending)
                if r + 1 < n_sub:
                    h_next = prologue(r + 1)
            a = (_silu(g) * u).astype(BF16)
            y = _dot(a, wd_ref[f])
            acc = y if acc is None else acc + y
        pending = (r, acc)
        h = h_next
    epilogue(*pending)


def _ffn(x, mod3, wg, wu, wd, ng, nb, *, j, ctx, tm):
    b_, t, d = x.shape
    return pl.pallas_call(
        functools.partial(_ffn_kernel, j=j, n_sub=max(tm // FFN_GROUP, 1)),
        out_shape=jax.ShapeDtypeStruct(x.shape, F32),
        grid=(b_, t // tm),
        in_specs=[_tok_spec(tm, d), _mod_spec(ctx), _resident(wg.shape), _resident(wu.shape), _resident(wd.shape),
                  _resident(ng.shape), _resident(nb.shape)],
        out_specs=_tok_spec(tm, d),
        compiler_params=_cparams("parallel", "parallel"),
        name="ffn_ctx" if ctx else "ffn",
    )(x, mod3, wg, wu, wd, ng, nb)


def _rope128(t, cos, sin_signed, lane_lo):
    swapped = jnp.where(lane_lo, pltpu.roll(t, 96, 1), pltpu.roll(t, 32, 1))
    return t * cos + swapped * sin_signed


def _inproj_kernel(*refs, rope):
    if rope:
        x_ref, mod_ref, w_ref, cos_ref, sin_ref, o_conv, o_lru, o_gdn, o_gab, o_q, o_k, o_v = refs
    else:
        x_ref, mod_ref, w_ref, o_conv, o_lru, o_gdn, o_gab, o_q, o_k, o_v = refs
    m = mod_ref[0]
    h = (_ln(x_ref[0]) * (1.0 + m[4:5]) + m[3:4]).astype(BF16)
    o_conv[0] = _dot(h, w_ref[:, 0:1024])
    o_lru[0] = _dot(h, w_ref[:, 1024:2048])
    o_gdn[0] = _dot(h, w_ref[:, 2048:4096])
    o_gab[0] = _dot(h, w_ref[:, 4096:4224])
    q = _dot(h, w_ref[:, 4224:4736])
    k = _dot(h, w_ref[:, 4736:4992])
    o_v[0] = _dot(h, w_ref[:, 4992:5248])
    scale = ATT_HEAD_DIM ** -0.5
    if rope:
        cos, sin = cos_ref[...], sin_ref[...]
        lane_lo = (lax.broadcasted_iota(jnp.int32, cos.shape, 1) % ATT_HEAD_DIM) < (ATT_HEAD_DIM // 2)
        for s in range(ATT_DIM // 128):
            o_q[0, :, 128 * s:128 * (s + 1)] = _rope128(q[:, 128 * s:128 * (s + 1)], cos, sin, lane_lo) * scale
        for s in range(2):
            o_k[0, :, 128 * s:128 * (s + 1)] = _rope128(k[:, 128 * s:128 * (s + 1)], cos, sin, lane_lo)
    else:
        o_q[0] = q * scale
        o_k[0] = k


_INPROJ_WIDTHS = (1024, 1024, 2048, 128, 512, 256, 256)


def _inproj(x, mod3, w_cat, rope_tabs, *, ctx, tm):
    b_, t, d = x.shape
    in_specs = [_tok_spec(tm, d), _mod_spec(ctx), _resident(w_cat.shape)]
    args = [x, mod3, w_cat]
    if rope_tabs is not None:
        in_specs += [pl.BlockSpec((tm, 128), lambda b, i: (i, 0))] * 2
        args += list(rope_tabs)
    return pl.pallas_call(
        functools.partial(_inproj_kernel, rope=rope_tabs is not None),
        out_shape=[jax.ShapeDtypeStruct((b_, t, w), F32) for w in _INPROJ_WIDTHS],
        grid=(b_, t // tm),
        in_specs=in_specs,
        out_specs=[_tok_spec(tm, w) for w in _INPROJ_WIDTHS],
        compiler_params=_cparams("parallel", "parallel"),
        name="inproj_ctx" if ctx else "inproj",
    )(*args)


def _conv_kernel(main_ref, prev_ref, next_ref, w_ref, cb_ref, g_ref, b_ref, o_ref, ubuf, *, tt, n_t, rb):
    i = pl.program_id(1)

    def glu(blk):
        return blk[:, :CONV_DIM] * jax.nn.sigmoid(blk[:, CONV_DIM:])

    n_rows = tt + 2 * CONV_HALO
    ubuf[0, 0:CONV_HALO] = jnp.where(i > 0, glu(prev_ref[0]), 0.0)
    ubuf[0, CONV_HALO:CONV_HALO + tt] = glu(main_ref[0])
    ubuf[0, CONV_HALO + tt:] = jnp.where(i < n_t - 1, glu(next_ref[0]), 0.0)
    for s in range(1, SUBLANES):
        ubuf[s, 0:n_rows - SUBLANES] = ubuf[0, s:s + n_rows - SUBLANES]
    base = CONV_HALO - (CONV_WIDTH - 1) // 2
    for r0 in range(0, tt, rb):
        acc = None
        for k in range(CONV_WIDTH):
            tile, s = divmod(base + k, SUBLANES)
            start = r0 + tile * SUBLANES
            term = ubuf[s, start:start + rb, :] * w_ref[k]
            acc = term if acc is None else acc + term
        y = _ln(acc + cb_ref[...]) * g_ref[...] + b_ref[...]
        o_ref[0, r0:r0 + rb, :] = _silu(y)


def _conformer(conv_in, w, cb, ng, nb, *, tt):
    b_, t, width = conv_in.shape
    n_t = t // tt
    hb = tt // CONV_HALO
    n_h = t // CONV_HALO
    rb = 2 * SUBLANES
    w = jnp.broadcast_to(w[:, None, :], (CONV_WIDTH, rb, CONV_DIM))
    return pl.pallas_call(
        functools.partial(_conv_kernel, tt=tt, n_t=n_t, rb=rb),
        out_shape=jax.ShapeDtypeStruct((b_, t, CONV_DIM), F32),
        grid=(b_, n_t),
        in_specs=[_tok_spec(tt, width),
                  pl.BlockSpec((1, CONV_HALO, width), lambda b, i: (b, jnp.maximum(i * hb - 1, 0), 0)),
                  pl.BlockSpec((1, CONV_HALO, width), lambda b, i: (b, jnp.minimum((i + 1) * hb, n_h - 1), 0)),
                  _resident(w.shape), _resident(cb.shape), _resident(ng.shape), _resident(nb.shape)],
        out_specs=_tok_spec(tt, CONV_DIM),
        scratch_shapes=[pltpu.VMEM((SUBLANES, tt + 2 * CONV_HALO, CONV_DIM), F32)],
        compiler_params=_cparams("parallel", "parallel"),
        name="conformer_conv",
    )(conv_in, conv_in, conv_in, w, cb, ng, nb)


def _halo_specs(tt, t, width, reverse):
    n_t = t // tt
    hb = tt // SUBLANES
    n_h = t // SUBLANES
    if reverse:
        main = pl.BlockSpec((1, tt, width), lambda b, i: (b, n_t - 1 - i, 0))
        halo = pl.BlockSpec((1, SUBLANES, width), lambda b, i: (b, jnp.minimum((n_t - i) * hb, n_h - 1), 0))
    else:
        main = pl.BlockSpec((1, tt, width), lambda b, i: (b, i, 0))
        halo = pl.BlockSpec((1, SUBLANES, width), lambda b, i: (b, jnp.maximum(i * hb - 1, 0), 0))
    return main, halo


def _fill_conv_buffer(xbuf, main, halo, first, tt, reverse):
    halo = jnp.where(first, 0.0, halo)
    if reverse:
        xbuf[0:tt] = main
        xbuf[tt:] = halo
        return lambda k: LRU_CONV - 1 - k
    xbuf[0:SUBLANES] = halo
    xbuf[SUBLANES:] = main
    return lambda k: SUBLANES - (LRU_CONV - 1) + k


def _lru_kernel(*refs, tt, reverse, final):
    if final:
        (main_ref, halo_ref, h0_ref, cw_ref, cb_ref, wr_ref, br_ref, wi_ref, bi_ref, lam_ref, hf_ref,
         o_ref, st_ref, xbuf, abuf, bbuf, carry) = refs
    else:
        (main_ref, halo_ref, h0_ref, cw_ref, cb_ref, wr_ref, br_ref, wi_ref, bi_ref, lam_ref,
         o_ref, st_ref, xbuf, abuf, bbuf, carry) = refs
    i = pl.program_id(1)

    @pl.when(i == 0)
    def _():
        carry[...] = h0_ref[0]

    off = _fill_conv_buffer(xbuf, main_ref[0, :, 0:LRU_DIM], halo_ref[0, :, 0:LRU_DIM], i == 0, tt, reverse)
    xc = cb_ref[...]
    for k in range(LRU_CONV):
        xc = xc + xbuf[off(k):off(k) + tt, :] * cw_ref[k:k + 1, :]
    xh = xc.astype(BF16)
    half = LRU_DIM // 2
    r_lin = jnp.concatenate([_dot(xh[:, :half], wr_ref[0]), _dot(xh[:, half:], wr_ref[1])], axis=1)
    i_lin = jnp.concatenate([_dot(xh[:, :half], wi_ref[0]), _dot(xh[:, half:], wi_ref[1])], axis=1)
    r = jax.nn.sigmoid(r_lin + br_ref[...])
    gi = jax.nn.sigmoid(i_lin + bi_ref[...])
    log_a = (-LRU_C) * r * _softplus(-lam_ref[...])
    a = jnp.exp(log_a)
    th = jnp.tanh(log_a)
    bb = jnp.sqrt(-2.0 * th / (1.0 - th)) * (gi * xc)

    n_g = tt // SUBLANES
    a3 = a.reshape(n_g, SUBLANES, LRU_DIM)
    b3 = bb.reshape(n_g, SUBLANES, LRU_DIM)
    row = lax.broadcasted_iota(jnp.int32, a3.shape, 1)
    for s in (1, 2, 4):
        if reverse:
            keep = row < SUBLANES - s
            a_sh = pltpu.roll(a3, SUBLANES - s, 1)
            b_sh = pltpu.roll(b3, SUBLANES - s, 1)
        else:
            keep = row >= s
            a_sh = pltpu.roll(a3, s, 1)
            b_sh = pltpu.roll(b3, s, 1)
        b3 = jnp.where(keep, a3 * b_sh + b3, b3)
        a3 = jnp.where(keep, a3 * a_sh, a3)
    abuf[...] = a3.reshape(tt, LRU_DIM)
    bbuf[...] = b3.reshape(tt, LRU_DIM)

    hprev = carry[...]
    order = range(n_g - 1, -1, -1) if reverse else range(n_g)
    edge = 0 if reverse else SUBLANES - 1
    for g in order:
        rows = slice(g * SUBLANES, (g + 1) * SUBLANES)
        hg = abuf[rows, :] * hprev + bbuf[rows, :]
        hprev = hg[edge:edge + 1, :]
        if final:
            gate = jax.nn.gelu(main_ref[0, rows, LRU_DIM:2 * LRU_DIM])
            o_ref[0, rows, :] = (hf_ref[0, rows, :] + hg) * gate
        else:
            o_ref[0, rows, :] = hg
    carry[...] = hprev
    st_ref[0] = hprev


def _lru(lru_in, h0, prm, h_fwd, *, tt, reverse):
    b_, t, width = lru_in.shape
    final = h_fwd is not None
    main, halo = _halo_specs(tt, t, width, reverse)
    in_specs = [main, halo, pl.BlockSpec((1, 1, LRU_DIM), lambda b, i: (b, 0, 0))] + [_resident(p.shape) for p in prm]
    args = [lru_in, lru_in, h0] + list(prm)
    out_tile = pl.BlockSpec((1, tt, LRU_DIM), main.index_map)
    if final:
        in_specs.append(out_tile)
        args.append(h_fwd)
    return pl.pallas_call(
        functools.partial(_lru_kernel, tt=tt, reverse=reverse, final=final),
        out_shape=[jax.ShapeDtypeStruct((b_, t, LRU_DIM), F32), jax.ShapeDtypeStruct((b_, 1, LRU_DIM), F32)],
        grid=(b_, t // tt),
        in_specs=in_specs,
        out_specs=[out_tile, pl.BlockSpec((1, 1, LRU_DIM), lambda b, i: (b, 0, 0))],
        scratch_shapes=[pltpu.VMEM((tt + SUBLANES, LRU_DIM), F32), pltpu.VMEM((tt, LRU_DIM), F32),
                        pltpu.VMEM((tt, LRU_DIM), F32), pltpu.VMEM((1, LRU_DIM), F32)],
        compiler_params=_cparams("parallel", "arbitrary"),
        name="rglru_rev" if reverse else "rglru_fwd",
    )(*args)


def _unit_tri_inverses(ms, ri, ci, eye):
    def same_block(log2_size):
        return jnp.right_shift(ri, log2_size) == jnp.right_shift(ci, log2_size)

    base = same_block(3)
    ps = [jnp.where(base, -m, 0.0) for m in ms]
    xs = [eye + p for p in ps]
    for _ in range(2):
        p16 = [p.astype(BF16) for p in ps]
        ps = [_dot(p, p) for p in p16]
        xs = [x + _dot(x.astype(BF16), p.astype(BF16)) for x, p in zip(xs, ps)]
    for log2_size in (3, 4, 5):
        level = same_block(log2_size + 1) & ~same_block(log2_size)
        x16 = [x.astype(BF16) for x in xs]
        ys = [_dot(x, jnp.where(level, m, 0.0).astype(BF16)) for x, m in zip(x16, ms)]
        xs = [x - _dot(y.astype(BF16), xb) for x, y, xb in zip(xs, ys, x16)]
    return xs


def _gdn_kernel(*refs, tt, direction, final):
    if final:
        (main_ref, halo_ref, gab_ref, s0_ref, cw_ref, alog_ref, dtb_ref, of_ref, gn_ref,
         o_ref, sfin_ref, xbuf, s_ref) = refs
    else:
        (main_ref, halo_ref, gab_ref, s0_ref, cw_ref, alog_ref, dtb_ref,
         o_ref, sfin_ref, xbuf, s_ref) = refs
    reverse = direction == 1
    i = pl.program_id(1)
    c_ = GDN_CHUNK
    hd = GDN_HEAD_DIM
    qkv_w = 3 * GDN_DIM

    @pl.when(i == 0)
    def _():
        s_ref[...] = s0_ref[0]

    off = _fill_conv_buffer(xbuf, main_ref[0, :, 0:qkv_w], halo_ref[0, :, 0:qkv_w], i == 0, tt, reverse)

    ri = lax.broadcasted_iota(jnp.int32, (c_, c_), 0)
    ci = lax.broadcasted_iota(jnp.int32, (c_, c_), 1)
    incl = (ci >= ri) if reverse else (ci <= ri)
    strict = (ci > ri) if reverse else (ci < ri)
    eye = (ri == ci).astype(F32)
    ones_tri = jnp.where(incl, 1.0, 0.0).astype(BF16)
    last = 0 if reverse else c_ - 1

    n_c = tt // c_
    chunks = list(range(n_c - 1, -1, -1) if reverse else range(n_c))
    heads = range(GDN_HEADS)

    pre = {}
    for c in chunks:
        r0 = c * c_
        acc = None
        for k in range(GDN_CONV):
            term = xbuf[r0 + off(k):r0 + off(k) + c_, :] * cw_ref[k:k + 1, :]
            acc = term if acc is None else acc + term
        u = _silu(acc)
        ga = gab_ref[0, r0:r0 + c_, :]
        g_all = -jnp.exp(alog_ref[...]) * _softplus(ga + dtb_ref[...])
        beta_all = jax.nn.sigmoid(ga)
        g_hi, g_lo = _split(g_all)
        g_lo2 = (g_all - g_hi.astype(F32) - g_lo.astype(F32)).astype(BF16)
        gcum = _dot(ones_tri, g_hi) + (_dot(ones_tri, g_lo) + _dot(ones_tri, g_lo2))
        gcum_t = gcum.T
        g_last = gcum[last:last + 1, :]
        for h in heads:
            col = direction * GDN_HEADS + h
            q = u[:, h * hd:(h + 1) * hd]
            k = u[:, GDN_DIM + h * hd:GDN_DIM + (h + 1) * hd]
            v = u[:, 2 * GDN_DIM + h * hd:2 * GDN_DIM + (h + 1) * hd]
            q = q * (lax.rsqrt(jnp.sum(q * q, axis=-1, keepdims=True) + NORM_EPS) * (hd ** -0.5))
            k = k * lax.rsqrt(jnp.sum(k * k, axis=-1, keepdims=True) + NORM_EPS)
            gc = gcum[:, col:col + 1]
            gl = g_last[:, col:col + 1]
            beta = beta_all[:, 2 * GDN_HEADS + col:2 * GDN_HEADS + col + 1]
            decay = jnp.exp(jnp.where(incl, gc - gcum_t[col:col + 1, :], -jnp.inf))
            kb = k * beta
            k16 = k.astype(BF16)
            eg = jnp.exp(gc)
            pre[c, h] = dict(
                m=jnp.where(strict, _dot_nt(kb.astype(BF16), k16) * decay, 0.0),
                attn=(_dot_nt(q.astype(BF16), k16) * decay).astype(BF16),
                rhs=jnp.concatenate([v * beta, kb * eg], axis=1).astype(BF16),
                qd=q * eg, ks=(k * jnp.exp(gl - gc)).astype(BF16), cd=jnp.exp(gl))
    keys = [(c, h) for c in chunks for h in heads]
    t_inv = _unit_tri_inverses([pre[key]["m"] for key in keys], ri, ci, eye)
    for key, t in zip(keys, t_inv):
        sol = _dot(t.astype(BF16), pre[key]["rhs"])
        pre[key]["u"] = sol[:, :hd]
        pre[key]["wq"] = jnp.concatenate([sol[:, hd:], pre[key]["qd"]], axis=0).astype(BF16)

    state = [s_ref[h] for h in heads]
    for c in chunks:
        r0 = c * c_
        wq = [_dot(pre[c, h]["wq"], state[h].astype(BF16)) for h in heads]
        vn16 = [(pre[c, h]["u"] - wq[h][:c_]).astype(BF16) for h in heads]
        outs = [wq[h][c_:] + _dot(pre[c, h]["attn"], vn16[h]) for h in heads]
        state = [state[h] * pre[c, h]["cd"] + _dot_tn(pre[c, h]["ks"], vn16[h]) for h in heads]
        for h in heads:
            cols = slice(h * hd, (h + 1) * hd)
            o = outs[h]
            if final:
                o = o + of_ref[0, r0:r0 + c_, cols]
                z = main_ref[0, r0:r0 + c_, qkv_w + h * hd:qkv_w + (h + 1) * hd]
                y = o * lax.rsqrt(jnp.mean(o * o, axis=-1, keepdims=True) + NORM_EPS) * gn_ref[...]
                o_ref[0, r0:r0 + c_, cols] = y * _silu(z)
            else:
                o_ref[0, r0:r0 + c_, cols] = o
    for h in heads:
        s_ref[h] = state[h]
    sfin_ref[0] = s_ref[...]


def _gdn(gdn_in, gab, s0, prm, o_fwd, gn, *, tt, direction):
    b_, t, width = gdn_in.shape
    final = o_fwd is not None
    reverse = direction == 1
    main, halo = _halo_specs(tt, t, width, reverse)
    st_spec = pl.BlockSpec((1, GDN_HEADS, GDN_HEAD_DIM, GDN_HEAD_DIM), lambda b, i: (b, 0, 0, 0))
    out_tile = pl.BlockSpec((1, tt, GDN_DIM), main.index_map)
    in_specs = [main, halo, pl.BlockSpec((1, tt, 128), main.index_map), st_spec] + [_resident(p.shape) for p in prm]
    args = [gdn_in, gdn_in, gab, s0] + list(prm)
    if final:
        in_specs += [out_tile, _resident(gn.shape)]
        args += [o_fwd, gn]
    return pl.pallas_call(
        functools.partial(_gdn_kernel, tt=tt, direction=direction, final=final),
        out_shape=[jax.ShapeDtypeStruct((b_, t, GDN_DIM), F32),
                   jax.ShapeDtypeStruct((b_, GDN_HEADS, GDN_HEAD_DIM, GDN_HEAD_DIM), F32)],
        grid=(b_, t // tt),
        in_specs=in_specs,
        out_specs=[out_tile, st_spec],
        scratch_shapes=[pltpu.VMEM((tt + SUBLANES, 3 * GDN_DIM), F32),
                        pltpu.VMEM((GDN_HEADS, GDN_HEAD_DIM, GDN_HEAD_DIM), F32)],
        compiler_params=_cparams("parallel", "arbitrary"),
        name="gdn_rev" if reverse else "gdn_fwd",
    )(*args)


def _att_kernel(*refs, n_steps, qb, window):
    if window:
        q_ref, kp_ref, kc_ref, kn_ref, vp_ref, vc_ref, vn_ref, kx_ref, vx_ref, sink_ref, o_ref = refs
    else:
        q_ref, kx_ref, vx_ref, sink_ref, o_ref = refs
    i = pl.program_id(1)
    blk = ATT_BLOCK
    group = ATT_HEADS // ATT_KV_HEADS
    lane = lax.broadcasted_iota(jnp.int32, (blk, 128), 1)
    lo = lane < ATT_HEAD_DIM
    n_ctx = kx_ref.shape[1]
    row_head = lax.broadcasted_iota(jnp.int32, (group * blk, 1), 0) // blk
    if window:
        r = lax.broadcasted_iota(jnp.int32, (group * blk, 3 * blk + n_ctx), 0) & (blk - 1)
        c = lax.broadcasted_iota(jnp.int32, (group * blk, 3 * blk + n_ctx), 1)
        inner = ((c >= blk) | (c >= r)) & ((c < 2 * blk) | (c >= 3 * blk) | (c - 2 * blk <= r))
        masks = []
        for sb in range(qb):
            m = inner
            if sb == 0:
                m = m & (c >= jnp.where(i > 0, 0, blk))
            if sb == qb - 1:
                m = m & ((c < 2 * blk) | (c >= jnp.where(i < n_steps - 1, 2 * blk, 3 * blk)))
            masks.append(m)
    chains = [(sb, j) for sb in range(qb) for j in range(ATT_KV_HEADS)]
    v_alls, scores, sinks = [], [], []
    for sb, j in chains:
        cols = slice(128 * j, 128 * (j + 1))
        if window:
            k_win = jnp.concatenate([kp_ref[0, :, cols], kc_ref[0, :, cols], kn_ref[0, :, cols]], axis=0)
            v_win = jnp.concatenate([vp_ref[0, :, cols], vc_ref[0, :, cols], vn_ref[0, :, cols]], axis=0)
            rows = slice(sb * blk, (sb + 3) * blk)
            k_all = jnp.concatenate([k_win[rows], kx_ref[0, :, cols]], axis=0)
            v_all = jnp.concatenate([v_win[rows], vx_ref[0, :, cols]], axis=0)
        else:
            k_all, v_all = kx_ref[0, :, cols], vx_ref[0, :, cols]
        v_alls.append(v_all.astype(BF16))
        q_rows = []
        sk = jnp.zeros((group * blk, 1), F32)
        for e in range(group):
            head = group * j + e
            qp = q_ref[0, sb * blk:(sb + 1) * blk, 128 * (head // 2):128 * (head // 2 + 1)]
            q_rows.append(jnp.where(lo if head % 2 == 0 else ~lo, qp, 0.0).astype(BF16))
            sk = jnp.where(row_head == e, sink_ref[head], sk)
        sinks.append(sk)
        s = _dot_nt(jnp.concatenate(q_rows, axis=0), k_all.astype(BF16))
        scores.append(jnp.where(masks[sb], s, NEG_BIG) if window else s)
    mxs = [jnp.maximum(jnp.max(s, axis=-1, keepdims=True), sk) for s, sk in zip(scores, sinks)]
    ps = [jnp.exp(s - mx) for s, mx in zip(scores, mxs)]
    dens = [jnp.sum(p, axis=-1, keepdims=True) + jnp.exp(sk - mx) for p, sk, mx in zip(ps, sinks, mxs)]
    outs = [_dot(p.astype(BF16), v_all) / den for p, v_all, den in zip(ps, v_alls, dens)]
    for (sb, j), o in zip(chains, outs):
        for pair in range(group // 2):
            slab = (group * j) // 2 + pair
            o_ref[0, sb * blk:(sb + 1) * blk, 128 * slab:128 * (slab + 1)] = jnp.where(
                lo, o[2 * pair * blk:(2 * pair + 1) * blk], o[(2 * pair + 1) * blk:(2 * pair + 2) * blk])


def _attention(q, k, v, kx, vx, sink, *, window):
    b_, t, _ = q.shape
    qb = ATT_QUERY_BLOCKS
    tile = qb * ATT_BLOCK
    n_steps = t // tile
    n_blk = t // ATT_BLOCK
    n_ctx = kx.shape[1]
    cur = lambda b, i: (b, i, 0)
    prev = lambda b, i: (b, jnp.maximum(i * qb - 1, 0), 0)
    nxt = lambda b, i: (b, jnp.minimum((i + 1) * qb, n_blk - 1), 0)
    kv_w = kx.shape[-1]
    ctx_spec = pl.BlockSpec((1, n_ctx, kv_w), lambda b, i: (b, 0, 0))
    in_specs = [pl.BlockSpec((1, tile, ATT_DIM), cur)]
    args = [q]
    if window:
        in_specs += [pl.BlockSpec((1, ATT_BLOCK, kv_w), prev), pl.BlockSpec((1, tile, kv_w), cur),
                     pl.BlockSpec((1, ATT_BLOCK, kv_w), nxt)] * 2
        args += [k, k, k, v, v, v]
    in_specs += [ctx_spec, ctx_spec, pl.BlockSpec(memory_space=pltpu.SMEM)]
    args += [kx, vx, sink]
    return pl.pallas_call(
        functools.partial(_att_kernel, n_steps=n_steps, qb=qb, window=window),
        out_shape=jax.ShapeDtypeStruct((b_, t, ATT_DIM), F32),
        grid=(b_, n_steps),
        in_specs=in_specs,
        out_specs=pl.BlockSpec((1, tile, ATT_DIM), cur),
        compiler_params=_cparams("parallel", "parallel"),
        name="window_attn" if window else "ctx_attn",
    )(*args)


def _merge_kernel(x_ref, mod_ref, yc_ref, yl_ref, yg_ref, ya_ref, wgate_ref, pc_ref, pl_ref, pg_ref, pa_ref, wout_ref,
                  g_ref, b_ref, o_ref):
    x = x_ref[0]
    m = mod_ref[0]
    h = (_ln(x) * (1.0 + m[4:5]) + m[3:4]).astype(BF16)
    mrg = None
    for n, (y_ref, p_ref) in enumerate(((yc_ref, pc_ref), (yl_ref, pl_ref), (yg_ref, pg_ref), (ya_ref, pa_ref))):
        gate = jax.nn.sigmoid(_dot(h, wgate_ref[:, n * D_MODEL:(n + 1) * D_MODEL]))
        term = gate * _dot(y_ref[0].astype(BF16), p_ref[...])
        mrg = term if mrg is None else mrg + term
    out = _dot(mrg.astype(BF16), wout_ref[...])
    s = DEEPNORM_ALPHA * x + m[5:6] * out
    o_ref[0] = _ln(s) * g_ref[...] + b_ref[...]


def _merge(x, mod3, ys, wgate, projs, wout, ng, nb, *, ctx, tm):
    b_, t, d = x.shape
    weights = [wgate, *projs, wout, ng, nb]
    return pl.pallas_call(
        _merge_kernel,
        out_shape=jax.ShapeDtypeStruct(x.shape, F32),
        grid=(b_, t // tm),
        in_specs=[_tok_spec(tm, d), _mod_spec(ctx)] + [_tok_spec(tm, y.shape[-1]) for y in ys]
                 + [_resident(w.shape) for w in weights],
        out_specs=_tok_spec(tm, d),
        compiler_params=_cparams("parallel", "parallel"),
        name="merge_ctx" if ctx else "merge",
    )(x, mod3, *ys, *weights)


def _block_diag_halves(w):
    n, bd, _ = w.shape
    per = n // 2
    on_diag = jnp.eye(per, dtype=jnp.bool_)[None, :, None, :, None]
    out = jnp.where(on_diag, w.reshape(2, per, bd, 1, bd), 0.0)
    return out.reshape(2, per * bd, per * bd).astype(BF16)


def _rope_tables(t):
    pos = jnp.arange(t, dtype=jnp.int32)
    row = (pos // GRID_W).astype(F32)
    col = (pos % GRID_W).astype(F32)
    n_freq = ATT_HEAD_DIM // 4
    inv = ROPE_BASE ** (-jnp.arange(n_freq, dtype=F32) / n_freq)
    ang = jnp.concatenate([row[:, None] * inv, col[:, None] * inv], axis=-1)
    cos, sin = jnp.cos(ang), jnp.sin(ang)
    cos128 = jnp.tile(cos, (1, 4))
    sin128 = jnp.tile(jnp.concatenate([-sin, sin], axis=-1), (1, 2))
    return cos128, sin128


def _layer_params(l, ffn_w_in, ffn_w_out, w_in, lru_w_r, lru_w_i, proj_conv, proj_lru, proj_gdn, proj_att, w_out):
    fc = D_FF // FF_CHUNKS
    ffn = []
    for f in range(2):
        wgu = ffn_w_in[l, f].astype(BF16)
        wg = wgu[:, :D_FF].reshape(D_MODEL, FF_CHUNKS, fc).transpose(1, 0, 2)
        wu = wgu[:, D_FF:].reshape(D_MODEL, FF_CHUNKS, fc).transpose(1, 0, 2)
        wd = ffn_w_out[l, f].astype(BF16).reshape(FF_CHUNKS, fc, D_MODEL)
        ffn.append((wg, wu, wd))
    w = w_in[l]
    kd = ATT_HEAD_DIM
    wk = w[:, _OFF[9]:_OFF[10]]
    wv = w[:, _OFF[10]:_OFF[11]]
    dup = lambda m: jnp.concatenate([m[:, :kd], m[:, :kd], m[:, kd:], m[:, kd:]], axis=1)
    gab = jnp.pad(w[:, _OFF[6]:_OFF[8]], ((0, 0), (0, 128 - 4 * GDN_HEADS)))
    w_cat = jnp.concatenate([w[:, _OFF[0]:_OFF[2]], w[:, _OFF[2]:_OFF[4]], w[:, _OFF[4]:_OFF[6]], gab,
                             w[:, _OFF[8]:_OFF[9]], dup(wk), dup(wv)], axis=1).astype(BF16)
    wgate = w[:, _OFF[11]:_OFF[12]].astype(BF16)
    projs = [p[l].astype(BF16) for p in (proj_conv, proj_lru, proj_gdn, proj_att)]
    lru_w = [(_block_diag_halves(lru_w_r[l, d]), _block_diag_halves(lru_w_i[l, d])) for d in range(2)]
    return dict(ffn=ffn, w_cat=w_cat, wgate=wgate, projs=projs, wout=w_out[l].astype(BF16), lru_w=lru_w)


def kernel(x, c, ctx, c_ctx, ada_w, ada_b, norm_g, norm_b, ffn_w_in, ffn_w_out, w_in, conv_w, conv_b, conv_norm_g, conv_norm_b, lru_conv_w, lru_conv_b, lru_w_r, lru_b_r, lru_w_i, lru_b_i, lru_lam, gdn_conv_w, gdn_a_log, gdn_dt_bias, gdn_norm_g, att_sink, proj_conv, proj_lru, proj_gdn, proj_att, w_out):
    b_, t, d = x.shape
    lc = ctx.shape[1]
    assert d == D_MODEL and t % FFN_TILE == 0 and lc == 256 and b_ <= CTX_MOD_ROW
    c_all = jnp.zeros((MOD_ROWS, d), F32).at[:b_].set(c).at[CTX_MOD_ROW].set(c_ctx)
    mod_all = _modulation(c_all, ada_w, ada_b)
    rope_tabs = _rope_tables(t)
    row2 = lambda v: v.reshape(1, -1)
    xc = ctx
    for l in range(DEPTH):
        last = l == DEPTH - 1
        p = _layer_params(l, ffn_w_in, ffn_w_out, w_in, lru_w_r, lru_w_i, proj_conv, proj_lru, proj_gdn, proj_att, w_out)
        mod3 = mod_all[l]
        ng = [row2(norm_g[l, j]) for j in range(3)]
        nb = [row2(norm_b[l, j]) for j in range(3)]

        x = _ffn(x, mod3, *p["ffn"][0], ng[0], nb[0], j=0, ctx=False, tm=FFN_TILE)
        xc = _ffn(xc, mod3, *p["ffn"][0], ng[0], nb[0], j=0, ctx=True, tm=lc)

        z = _inproj(x, mod3, p["w_cat"], rope_tabs, ctx=False, tm=PROJ_TILE)
        zc = _inproj(xc, mod3, p["w_cat"], None, ctx=True, tm=lc)
        conv_in, lru_in, gdn_in, gab, q, k, v = z
        conv_c, lru_c, gdn_c, gab_c, q_c, k_c, v_c = zc

        conv_prm = (conv_w[l], row2(conv_b[l]), row2(conv_norm_g[l]), row2(conv_norm_b[l]))
        y_conv = _conformer(conv_in, *conv_prm, tt=SEQ_TILE)

        lru_prm = [(lru_conv_w[l, dd], row2(lru_conv_b[l, dd]), p["lru_w"][dd][0], row2(lru_b_r[l, dd]),
                    p["lru_w"][dd][1], row2(lru_b_i[l, dd]), row2(lru_lam[l, dd])) for dd in range(2)]
        h0 = jnp.zeros((b_, 1, LRU_DIM), F32)
        hc_f, st_f = _lru(lru_c, h0, lru_prm[0], None, tt=lc, reverse=False)
        hl_f, _ = _lru(lru_in, st_f, lru_prm[0], None, tt=SEQ_TILE, reverse=False)
        y_lru_c, st_r = _lru(lru_c, h0, lru_prm[1], None if last else hc_f, tt=lc, reverse=True)
        y_lru, _ = _lru(lru_in, st_r, lru_prm[1], hl_f, tt=SEQ_TILE, reverse=True)

        pad8 = lambda a: jnp.pad(a.reshape(1, -1), ((0, 0), (0, 128 - 2 * GDN_HEADS)))
        alog_row, dtb_row = pad8(gdn_a_log[l]), pad8(gdn_dt_bias[l])
        gdn_prm = [(gdn_conv_w[l, dd], alog_row, dtb_row) for dd in range(2)]
        gn = row2(gdn_norm_g[l])
        s0 = jnp.zeros((b_, GDN_HEADS, GDN_HEAD_DIM, GDN_HEAD_DIM), F32)
        oc_f, sg_f = _gdn(gdn_c, gab_c, s0, gdn_prm[0], None, None, tt=lc, direction=0)
        ol_f, _ = _gdn(gdn_in, gab, sg_f, gdn_prm[0], None, None, tt=SEQ_TILE, direction=0)
        y_gdn_c, sg_r = _gdn(gdn_c, gab_c, s0, gdn_prm[1], None if last else oc_f, gn, tt=lc, direction=1)
        y_gdn, _ = _gdn(gdn_in, gab, sg_r, gdn_prm[1], ol_f, gn, tt=SEQ_TILE, direction=1)

        y_att = _attention(q, k, v, k_c, v_c, att_sink[l], window=True)

        x = _merge(x, mod3, (y_conv, y_lru, y_gdn, y_att), p["wgate"], p["projs"], p["wout"], ng[1], nb[1],
                   ctx=False, tm=PROJ_TILE)
        x = _ffn(x, mod3, *p["ffn"][1], ng[2], nb[2], j=2, ctx=False, tm=FFN_TILE)
        if not last:
            y_conv_c = _conformer(conv_c, *conv_prm, tt=lc)
            y_att_c = _attention(q_c, None, None, k_c, v_c, att_sink[l], window=False)
            xc = _merge(xc, mod3, (y_conv_c, y_lru_c, y_gdn_c, y_att_c), p["wgate"], p["projs"], p["wout"],
                        ng[1], nb[1], ctx=True, tm=lc)
            xc = _ffn(xc, mod3, *p["ffn"][1], ng[2], nb[2], j=2, ctx=True, tm=lc)
    return x
```

```python
import functools

import numpy as np
import jax
import jax.numpy as jnp
from jax import lax
from jax.experimental import pallas as pl
from jax.experimental.pallas import tpu as pltpu

F32 = jnp.float32
BF16 = jnp.bfloat16

D_MODEL = 1024
DEPTH = 2
GRID_W = 64
CONV_DIM = 512
CONV_WIDTH = 31
LRU_DIM = 512
LRU_BLOCKS = 8
LRU_CONV = 4
LRU_C = 8.0
GDN_HEADS = 4
GDN_HEAD_DIM = 128
GDN_DIM = GDN_HEADS * GDN_HEAD_DIM
GDN_CONV = 4
GDN_CHUNK = 64
ATT_HEADS = 8
ATT_KV_HEADS = 2
ATT_HEAD_DIM = 64
ATT_DIM = ATT_HEADS * ATT_HEAD_DIM
ATT_BLOCK = 128
ROPE_BASE = 10000.0
N_BRANCH = 4
D_FF = 2816
LN_EPS = 1e-5
NORM_EPS = 1e-6
DEEPNORM_ALPHA = (2 * DEPTH) ** 0.25

CTX_MOD_ROW = 4
MOD_ROWS = 8
FF_CHUNKS = 11
SUBLANES = 8
CONV_HALO = 16
VMEM_LIMIT = 56 * 1024 * 1024
FFN_GROUP = 512
FFN_TILE = 2 * FFN_GROUP
PROJ_TILE = 512
SEQ_TILE = 512
GDN_TILE = 256
ATT_QUERY_BLOCKS = 2
NEG_BIG = -1e30

_OFF = np.cumsum([0, CONV_DIM, CONV_DIM, LRU_DIM, LRU_DIM, 3 * GDN_DIM, GDN_DIM, 2 * GDN_HEADS, 2 * GDN_HEADS,
                  ATT_DIM, ATT_KV_HEADS * ATT_HEAD_DIM, ATT_KV_HEADS * ATT_HEAD_DIM, N_BRANCH * D_MODEL]).tolist()


def _cparams(*sem):
    return pltpu.CompilerParams(dimension_semantics=sem, vmem_limit_bytes=VMEM_LIMIT)


def _resident(shape):
    nd = len(shape)
    return pl.BlockSpec(tuple(shape), lambda *_: (0,) * nd, pipeline_mode=pl.Buffered(1))


def _ln(x):
    mu = jnp.mean(x, axis=-1, keepdims=True)
    xc = x - mu
    var = jnp.mean(xc * xc, axis=-1, keepdims=True)
    return xc * lax.rsqrt(var + LN_EPS)


def _silu(x):
    return x * jax.nn.sigmoid(x)


def _softplus(x):
    return jnp.maximum(x, 0.0) + jnp.log1p(jnp.exp(-jnp.abs(x)))


def _dot(a, b):
    return jnp.dot(a, b, preferred_element_type=F32)


def _dot_nt(a, b):
    return lax.dot_general(a, b, (((1,), (1,)), ((), ())), preferred_element_type=F32)


def _dot_tn(a, b):
    return lax.dot_general(a, b, (((0,), (0,)), ((), ())), preferred_element_type=F32)


def _split(a):
    hi = a.astype(BF16)
    return hi, (a - hi.astype(F32)).astype(BF16)


def _mod_spec(ctx):
    if ctx:
        return pl.BlockSpec((1, 9, D_MODEL), lambda b, i: (CTX_MOD_ROW, 0, 0))
    return pl.BlockSpec((1, 9, D_MODEL), lambda b, i: (b, 0, 0))


def _tok_spec(tm, width):
    return pl.BlockSpec((1, tm, width), lambda b, i: (b, i, 0))


def _mod_kernel(c_ref, w_ref, b_ref, o_ref):
    s = _silu(c_ref[...]).astype(BF16)
    o_ref[0] = _dot(s, w_ref[0].astype(BF16)) + b_ref[0]


def _modulation(c_all, ada_w, ada_b):
    n_l, d, n = ada_w.shape
    tn = 1024
    out = pl.pallas_call(
        _mod_kernel,
        out_shape=jax.ShapeDtypeStruct((n_l, MOD_ROWS, n), F32),
        grid=(n_l, n // tn),
        in_specs=[pl.BlockSpec((MOD_ROWS, d), lambda l, j: (0, 0)),
                  pl.BlockSpec((1, d, tn), lambda l, j: (l, 0, j)),
                  pl.BlockSpec((1, 1, tn), lambda l, j: (l, 0, j))],
        out_specs=pl.BlockSpec((1, MOD_ROWS, tn), lambda l, j: (l, 0, j)),
        compiler_params=_cparams("arbitrary", "arbitrary"),
        name="adaln_mod",
    )(c_all, ada_w, ada_b.reshape(n_l, 1, n))
    return out.reshape(n_l, MOD_ROWS, 9, d)


def _ffn_kernel(x_ref, mod_ref, wgu_ref, wd_ref, g_ref, b_ref, o_ref, *, j, n_sub):
    m = mod_ref[0]
    shift, scale, gate = m[3 * j:3 * j + 1], m[3 * j + 1:3 * j + 2], m[3 * j + 2:3 * j + 3]
    rows = x_ref.shape[1] // n_sub
    fc = D_FF // FF_CHUNKS
    for r in range(n_sub):
        rs = slice(r * rows, (r + 1) * rows)
        x = x_ref[0, rs]
        h = (_ln(x) * (1.0 + scale) + shift).astype(BF16)
        acc = None
        for f in range(FF_CHUNKS):
            g = _dot(h, wgu_ref[:, f * fc:(f + 1) * fc])
            u = _dot(h, wgu_ref[:, D_FF + f * fc:D_FF + (f + 1) * fc])
            a = (_silu(g) * u).astype(BF16)
            y = _dot(a, wd_ref[f * fc:(f + 1) * fc, :])
            acc = y if acc is None else acc + y
        s = DEEPNORM_ALPHA * x + (0.5 * gate) * acc
        o_ref[0, rs] = _ln(s) * g_ref[...] + b_ref[...]


def _ffn(x, mod3, wgu, wd, ng, nb, *, j, ctx, tm):
    b_, t, d = x.shape
    return pl.pallas_call(
        functools.partial(_ffn_kernel, j=j, n_sub=max(tm // FFN_GROUP, 1)),
        out_shape=jax.ShapeDtypeStruct(x.shape, F32),
        grid=(b_, t // tm),
        in_specs=[_tok_spec(tm, d), _mod_spec(ctx), _resident(wgu.shape), _resident(wd.shape),
                  _resident(ng.shape), _resident(nb.shape)],
        out_specs=_tok_spec(tm, d),
        compiler_params=_cparams("parallel", "parallel"),
        name="ffn_ctx" if ctx else "ffn",
    )(x, mod3, wgu, wd, ng, nb)


def _rope128(t, cos, sin_signed, lane_lo):
    swapped = jnp.where(lane_lo, pltpu.roll(t, 96, 1), pltpu.roll(t, 32, 1))
    return t * cos + swapped * sin_signed


def _inproj_kernel(*refs, rope):
    if rope:
        x_ref, mod_ref, w_ref, cos_ref, sin_ref, o_conv, o_lru, o_gdn, o_gab, o_q, o_k, o_v = refs
    else:
        x_ref, mod_ref, w_ref, o_conv, o_lru, o_gdn, o_gab, o_q, o_k, o_v = refs
    m = mod_ref[0]
    h = (_ln(x_ref[0]) * (1.0 + m[4:5]) + m[3:4]).astype(BF16)
    o_conv[0] = _dot(h, w_ref[:, 0:1024])
    o_lru[0] = _dot(h, w_ref[:, 1024:2048])
    o_gdn[0] = _dot(h, w_ref[:, 2048:4096])
    o_gab[0] = _dot(h, w_ref[:, 4096:4224])
    q = _dot(h, w_ref[:, 4224:4736])
    k = _dot(h, w_ref[:, 4736:4992])
    o_v[0] = _dot(h, w_ref[:, 4992:5248])
    scale = ATT_HEAD_DIM ** -0.5
    if rope:
        cos, sin = cos_ref[...], sin_ref[...]
        lane_lo = (lax.broadcasted_iota(jnp.int32, cos.shape, 1) % ATT_HEAD_DIM) < (ATT_HEAD_DIM // 2)
        for s in range(ATT_DIM // 128):
            o_q[0, :, 128 * s:128 * (s + 1)] = _rope128(q[:, 128 * s:128 * (s + 1)], cos, sin, lane_lo) * scale
        for s in range(2):
            o_k[0, :, 128 * s:128 * (s + 1)] = _rope128(k[:, 128 * s:128 * (s + 1)], cos, sin, lane_lo)
    else:
        o_q[0] = q * scale
        o_k[0] = k


_INPROJ_WIDTHS = (1024, 1024, 2048, 128, 512, 256, 256)


def _inproj(x, mod3, w_cat, rope_tabs, *, ctx, tm):
    b_, t, d = x.shape
    in_specs = [_tok_spec(tm, d), _mod_spec(ctx), _resident(w_cat.shape)]
    args = [x, mod3, w_cat]
    if rope_tabs is not None:
        in_specs += [pl.BlockSpec((tm, 128), lambda b, i: (i, 0))] * 2
        args += list(rope_tabs)
    return pl.pallas_call(
        functools.partial(_inproj_kernel, rope=rope_tabs is not None),
        out_shape=[jax.ShapeDtypeStruct((b_, t, w), F32) for w in _INPROJ_WIDTHS],
        grid=(b_, t // tm),
        in_specs=in_specs,
        out_specs=[_tok_spec(tm, w) for w in _INPROJ_WIDTHS],
        compiler_params=_cparams("parallel", "parallel"),
        name="inproj_ctx" if ctx else "inproj",
    )(*args)


def _conv_kernel(main_ref, prev_ref, next_ref, w_ref, cb_ref, g_ref, b_ref, o_ref, ubuf, *, tt, n_t, rb):
    i = pl.program_id(1)

    def glu(blk):
        return blk[:, :CONV_DIM] * jax.nn.sigmoid(blk[:, CONV_DIM:])

    n_rows = tt + 2 * CONV_HALO
    ubuf[0, 0:CONV_HALO] = jnp.where(i > 0, glu(prev_ref[0]), 0.0)
    ubuf[0, CONV_HALO:CONV_HALO + tt] = glu(main_ref[0])
    ubuf[0, CONV_HALO + tt:] = jnp.where(i < n_t - 1, glu(next_ref[0]), 0.0)
    for s in range(1, SUBLANES):
        ubuf[s, 0:n_rows - SUBLANES] = ubuf[0, s:s + n_rows - SUBLANES]
    base = CONV_HALO - (CONV_WIDTH - 1) // 2
    for r0 in range(0, tt, rb):
        acc = None
        for k in range(CONV_WIDTH):
            tile, s = divmod(base + k, SUBLANES)
            start = r0 + tile * SUBLANES
            term = ubuf[s, start:start + rb, :] * w_ref[k]
            acc = term if acc is None else acc + term
        y = _ln(acc + cb_ref[...]) * g_ref[...] + b_ref[...]
        o_ref[0, r0:r0 + rb, :] = _silu(y)


def _conformer(conv_in, w, cb, ng, nb, *, tt):
    b_, t, width = conv_in.shape
    n_t = t // tt
    hb = tt // CONV_HALO
    n_h = t // CONV_HALO
    rb = 2 * SUBLANES
    w = jnp.broadcast_to(w[:, None, :], (CONV_WIDTH, rb, CONV_DIM))
    return pl.pallas_call(
        functools.partial(_conv_kernel, tt=tt, n_t=n_t, rb=rb),
        out_shape=jax.ShapeDtypeStruct((b_, t, CONV_DIM), F32),
        grid=(b_, n_t),
        in_specs=[_tok_spec(tt, width),
                  pl.BlockSpec((1, CONV_HALO, width), lambda b, i: (b, jnp.maximum(i * hb - 1, 0), 0)),
                  pl.BlockSpec((1, CONV_HALO, width), lambda b, i: (b, jnp.minimum((i + 1) * hb, n_h - 1), 0)),
                  _resident(w.shape), _resident(cb.shape), _resident(ng.shape), _resident(nb.shape)],
        out_specs=_tok_spec(tt, CONV_DIM),
        scratch_shapes=[pltpu.VMEM((SUBLANES, tt + 2 * CONV_HALO, CONV_DIM), F32)],
        compiler_params=_cparams("parallel", "parallel"),
        name="conformer_conv",
    )(conv_in, conv_in, conv_in, w, cb, ng, nb)


def _halo_specs(tt, t, width, reverse):
    n_t = t // tt
    hb = tt // SUBLANES
    n_h = t // SUBLANES
    if reverse:
        main = pl.BlockSpec((1, tt, width), lambda b, i: (b, n_t - 1 - i, 0))
        halo = pl.BlockSpec((1, SUBLANES, width), lambda b, i: (b, jnp.minimum((n_t - i) * hb, n_h - 1), 0))
    else:
        main = pl.BlockSpec((1, tt, width), lambda b, i: (b, i, 0))
        halo = pl.BlockSpec((1, SUBLANES, width), lambda b, i: (b, jnp.maximum(i * hb - 1, 0), 0))
    return main, halo


def _fill_conv_buffer(xbuf, main, halo, first, tt, reverse):
    halo = jnp.where(first, 0.0, halo)
    if reverse:
        xbuf[0:tt] = main
        xbuf[tt:] = halo
        return lambda k: LRU_CONV - 1 - k
    xbuf[0:SUBLANES] = halo
    xbuf[SUBLANES:] = main
    return lambda k: SUBLANES - (LRU_CONV - 1) + k


def _lru_kernel(*refs, tt, reverse, final):
    if final:
        (main_ref, halo_ref, h0_ref, cw_ref, cb_ref, wr_ref, br_ref, wi_ref, bi_ref, lam_ref, hf_ref,
         o_ref, st_ref, xbuf, abuf, bbuf, carry) = refs
    else:
        (main_ref, halo_ref, h0_ref, cw_ref, cb_ref, wr_ref, br_ref, wi_ref, bi_ref, lam_ref,
         o_ref, st_ref, xbuf, abuf, bbuf, carry) = refs
    i = pl.program_id(1)

    @pl.when(i == 0)
    def _():
        carry[...] = h0_ref[0]

    off = _fill_conv_buffer(xbuf, main_ref[0, :, 0:LRU_DIM], halo_ref[0, :, 0:LRU_DIM], i == 0, tt, reverse)
    xc = cb_ref[...]
    for k in range(LRU_CONV):
        xc = xc + xbuf[off(k):off(k) + tt, :] * cw_ref[k:k + 1, :]
    xh = xc.astype(BF16)
    half = LRU_DIM // 2
    r_lin = jnp.concatenate([_dot(xh[:, :half], wr_ref[0]), _dot(xh[:, half:], wr_ref[1])], axis=1)
    i_lin = jnp.concatenate([_dot(xh[:, :half], wi_ref[0]), _dot(xh[:, half:], wi_ref[1])], axis=1)
    r = jax.nn.sigmoid(r_lin + br_ref[...])
    gi = jax.nn.sigmoid(i_lin + bi_ref[...])
    log_a = (-LRU_C) * r * _softplus(-lam_ref[...])
    a = jnp.exp(log_a)
    th = jnp.tanh(log_a)
    bb = jnp.sqrt(-2.0 * th / (1.0 - th)) * (gi * xc)

    n_g = tt // SUBLANES
    a3 = a.reshape(n_g, SUBLANES, LRU_DIM)
    b3 = bb.reshape(n_g, SUBLANES, LRU_DIM)
    row = lax.broadcasted_iota(jnp.int32, a3.shape, 1)
    for s in (1, 2, 4):
        if reverse:
            keep = row < SUBLANES - s
            a_sh = pltpu.roll(a3, SUBLANES - s, 1)
            b_sh = pltpu.roll(b3, SUBLANES - s, 1)
        else:
            keep = row >= s
            a_sh = pltpu.roll(a3, s, 1)
            b_sh = pltpu.roll(b3, s, 1)
        b3 = jnp.where(keep, a3 * b_sh + b3, b3)
        a3 = jnp.where(keep, a3 * a_sh, a3)
    abuf[...] = a3.reshape(tt, LRU_DIM)
    bbuf[...] = b3.reshape(tt, LRU_DIM)

    hprev = carry[...]
    order = range(n_g - 1, -1, -1) if reverse else range(n_g)
    edge = 0 if reverse else SUBLANES - 1
    for g in order:
        rows = slice(g * SUBLANES, (g + 1) * SUBLANES)
        hg = abuf[rows, :] * hprev + bbuf[rows, :]
        hprev = hg[edge:edge + 1, :]
        if final:
            gate = jax.nn.gelu(main_ref[0, rows, LRU_DIM:2 * LRU_DIM])
            o_ref[0, rows, :] = (hf_ref[0, rows, :] + hg) * gate
        else:
            o_ref[0, rows, :] = hg
    carry[...] = hprev
    st_ref[0] = hprev


def _lru(lru_in, h0, prm, h_fwd, *, tt, reverse):
    b_, t, width = lru_in.shape
    final = h_fwd is not None
    main, halo = _halo_specs(tt, t, width, reverse)
    in_specs = [main, halo, pl.BlockSpec((1, 1, LRU_DIM), lambda b, i: (b, 0, 0))] + [_resident(p.shape) for p in prm]
    args = [lru_in, lru_in, h0] + list(prm)
    out_tile = pl.BlockSpec((1, tt, LRU_DIM), main.index_map)
    if final:
        in_specs.append(out_tile)
        args.append(h_fwd)
    return pl.pallas_call(
        functools.partial(_lru_kernel, tt=tt, reverse=reverse, final=final),
        out_shape=[jax.ShapeDtypeStruct((b_, t, LRU_DIM), F32), jax.ShapeDtypeStruct((b_, 1, LRU_DIM), F32)],
        grid=(b_, t // tt),
        in_specs=in_specs,
        out_specs=[out_tile, pl.BlockSpec((1, 1, LRU_DIM), lambda b, i: (b, 0, 0))],
        scratch_shapes=[pltpu.VMEM((tt + SUBLANES, LRU_DIM), F32), pltpu.VMEM((tt, LRU_DIM), F32),
                        pltpu.VMEM((tt, LRU_DIM), F32), pltpu.VMEM((1, LRU_DIM), F32)],
        compiler_params=_cparams("parallel", "arbitrary"),
        name="rglru_rev" if reverse else "rglru_fwd",
    )(*args)


def _interleave(*streams):
    live = list(streams)
    while live:
        still = []
        for gen, rate in live:
            for _ in range(rate):
                if next(gen, StopIteration) is StopIteration:
                    break
            else:
                still.append((gen, rate))
        live = still


def _unit_tri_inverses(ms, ri, ci, eye, out):
    def same_block(log2_size):
        return jnp.right_shift(ri, log2_size) == jnp.right_shift(ci, log2_size)

    base = same_block(3)
    ps = [jnp.where(base, -m, 0.0) for m in ms]
    xs = [eye + p for p in ps]
    for _ in range(2):
        p16 = [p.astype(BF16) for p in ps]
        ps = [_dot(p, p) for p in p16]
        yield
        xs = [x + _dot(x.astype(BF16), p.astype(BF16)) for x, p in zip(xs, ps)]
        yield
    for log2_size in (3, 4, 5):
        level = same_block(log2_size + 1) & ~same_block(log2_size)
        x16 = [x.astype(BF16) for x in xs]
        ys = [_dot(x, jnp.where(level, m, 0.0).astype(BF16)) for x, m in zip(x16, ms)]
        yield
        xs = [x - _dot(y.astype(BF16), xb) for x, y, xb in zip(xs, ys, x16)]
        yield
    out.extend(xs)


def _gdn_kernel(*refs, tt, nb, direction, final):
    if final:
        (main_ref, halo_ref, gab_ref, s0_ref, cw_ref, alog_ref, dtb_ref, of_ref, gn_ref,
         o_ref, sfin_ref, xbuf, s_ref) = refs
    else:
        (main_ref, halo_ref, gab_ref, s0_ref, cw_ref, alog_ref, dtb_ref,
         o_ref, sfin_ref, xbuf, s_ref) = refs
    reverse = direction == 1
    i = pl.program_id(0)
    c_ = GDN_CHUNK
    hd = GDN_HEAD_DIM
    qkv_w = 3 * GDN_DIM

    @pl.when(i == 0)
    def _():
        s_ref[...] = s0_ref[...]

    for b in range(nb):
        off = _fill_conv_buffer(xbuf.at[b], main_ref[b, :, 0:qkv_w], halo_ref[b, :, 0:qkv_w], i == 0, tt, reverse)

    ri = lax.broadcasted_iota(jnp.int32, (c_, c_), 0)
    ci = lax.broadcasted_iota(jnp.int32, (c_, c_), 1)
    incl = (ci >= ri) if reverse else (ci <= ri)
    strict = (ci > ri) if reverse else (ci < ri)
    eye = (ri == ci).astype(F32)
    ones_tri = jnp.where(incl, 1.0, 0.0).astype(BF16)
    last = 0 if reverse else c_ - 1

    n_c = tt // c_
    chunks = list(range(n_c - 1, -1, -1) if reverse else range(n_c))
    heads = range(GDN_HEADS)

    batch = range(nb)
    lanes = [(b, h) for b in batch for h in heads]

    def elementwise_stage(c, pre):
        r0 = c * c_
        for b in batch:
            acc = None
            for k in range(GDN_CONV):
                term = xbuf[b, r0 + off(k):r0 + off(k) + c_, :] * cw_ref[k:k + 1, :]
                acc = term if acc is None else acc + term
            u = _silu(acc)
            yield
            ga = gab_ref[b, r0:r0 + c_, :]
            g_all = -jnp.exp(alog_ref[...]) * _softplus(ga + dtb_ref[...])
            beta_all = jax.nn.sigmoid(ga)
            g_hi, g_lo = _split(g_all)
            g_lo2 = (g_all - g_hi.astype(F32) - g_lo.astype(F32)).astype(BF16)
            gcum = _dot(ones_tri, g_hi) + (_dot(ones_tri, g_lo) + _dot(ones_tri, g_lo2))
            gcum_t = gcum.T
            g_last = gcum[last:last + 1, :]
            for h in heads:
                col = direction * GDN_HEADS + h
                q, k, v = (u[:, part * GDN_DIM + h * hd:part * GDN_DIM + (h + 1) * hd] for part in range(3))
                q = q * (lax.rsqrt(jnp.sum(q * q, axis=-1, keepdims=True) + NORM_EPS) * (hd ** -0.5))
                k = k * lax.rsqrt(jnp.sum(k * k, axis=-1, keepdims=True) + NORM_EPS)
                gc = gcum[:, col:col + 1]
                gl = g_last[:, col:col + 1]
                beta = beta_all[:, 2 * GDN_HEADS + col:2 * GDN_HEADS + col + 1]
                decay = jnp.exp(jnp.where(incl, gc - gcum_t[col:col + 1, :], -jnp.inf))
                kb = k * beta
                k16 = k.astype(BF16)
                eg = jnp.exp(gc)
                pre[b, h] = dict(
                    m=jnp.where(strict, _dot_nt(kb.astype(BF16), k16) * decay, 0.0),
                    attn=(_dot_nt(q.astype(BF16), k16) * decay).astype(BF16),
                    rhs=jnp.concatenate([v * beta, kb * eg], axis=1).astype(BF16),
                    qd=q * eg, ks=(k * jnp.exp(gl - gc)).astype(BF16), cd=jnp.exp(gl))
                yield

    def solve_stage(pre):
        t_inv = []
        yield from _unit_tri_inverses([pre[bh]["m"] for bh in lanes], ri, ci, eye, t_inv)
        for bh, t in zip(lanes, t_inv):
            sol = _dot(t.astype(BF16), pre[bh]["rhs"])
            pre[bh]["u"] = sol[:, :hd]
            pre[bh]["wq"] = jnp.concatenate([sol[:, hd:], pre[bh]["qd"]], axis=0).astype(BF16)
            if bh[1] == GDN_HEADS - 1:
                yield

    def recurrence_stage(c, cur, state):
        r0 = c * c_
        wq = {bh: _dot(cur[bh]["wq"], state[bh].astype(BF16)) for bh in lanes}
        yield
        vn16 = {bh: (cur[bh]["u"] - wq[bh][:c_]).astype(BF16) for bh in lanes}
        outs = {bh: wq[bh][c_:] + _dot(cur[bh]["attn"], vn16[bh]) for bh in lanes}
        yield
        state.update({bh: state[bh] * cur[bh]["cd"] + _dot_tn(cur[bh]["ks"], vn16[bh]) for bh in lanes})
        yield
        for b, h in lanes:
            cols = slice(h * hd, (h + 1) * hd)
            o = outs[b, h]
            if final:
                o = o + of_ref[b, r0:r0 + c_, cols]
                z = main_ref[b, r0:r0 + c_, qkv_w + h * hd:qkv_w + (h + 1) * hd]
                y = o * lax.rsqrt(jnp.mean(o * o, axis=-1, keepdims=True) + NORM_EPS) * gn_ref[...]
                o_ref[b, r0:r0 + c_, cols] = y * _silu(z)
            else:
                o_ref[b, r0:r0 + c_, cols] = o
            if h == GDN_HEADS - 1:
                yield

    state = {bh: s_ref[bh] for bh in lanes}
    pre = {n: {} for n in range(n_c)}
    for n in range(n_c + 2):
        streams = []
        if n < n_c:
            streams.append((elementwise_stage(chunks[n], pre[n]), 3))
        if 1 <= n <= n_c:
            streams.append((solve_stage(pre[n - 1]), 2))
        if n >= 2:
            streams.append((recurrence_stage(chunks[n - 2], pre.pop(n - 2), state), 1))
        _interleave(*streams)
    for bh in lanes:
        s_ref[bh] = state[bh]
    sfin_ref[...] = s_ref[...]


def _gdn(gdn_in, gab, s0, prm, o_fwd, gn, *, tt, direction):
    b_, t, width = gdn_in.shape
    final = o_fwd is not None
    reverse = direction == 1
    n_t = t // tt
    hb = tt // SUBLANES
    n_h = t // SUBLANES
    if reverse:
        tile = lambda i: (0, n_t - 1 - i, 0)
        halo_map = lambda i: (0, jnp.minimum((n_t - i) * hb, n_h - 1), 0)
    else:
        tile = lambda i: (0, i, 0)
        halo_map = lambda i: (0, jnp.maximum(i * hb - 1, 0), 0)
    st_shape = (b_, GDN_HEADS, GDN_HEAD_DIM, GDN_HEAD_DIM)
    st_spec = pl.BlockSpec(st_shape, lambda i: (0, 0, 0, 0))
    out_tile = pl.BlockSpec((b_, tt, GDN_DIM), tile)
    in_specs = [pl.BlockSpec((b_, tt, width), tile), pl.BlockSpec((b_, SUBLANES, width), halo_map),
                pl.BlockSpec((b_, tt, 128), tile), st_spec] + [_resident(p.shape) for p in prm]
    args = [gdn_in, gdn_in, gab, s0] + list(prm)
    if final:
        in_specs += [out_tile, _resident(gn.shape)]
        args += [o_fwd, gn]
    return pl.pallas_call(
        functools.partial(_gdn_kernel, tt=tt, nb=b_, direction=direction, final=final),
        out_shape=[jax.ShapeDtypeStruct((b_, t, GDN_DIM), F32), jax.ShapeDtypeStruct(st_shape, F32)],
        grid=(n_t,),
        in_specs=in_specs,
        out_specs=[out_tile, st_spec],
        scratch_shapes=[pltpu.VMEM((b_, tt + SUBLANES, 3 * GDN_DIM), F32), pltpu.VMEM(st_shape, F32)],
        compiler_params=_cparams("arbitrary"),
        name="gdn_rev" if reverse else "gdn_fwd",
    )(*args)


def _att_kernel(*refs, n_steps, qb, window):
    if window:
        q_ref, kp_ref, kc_ref, kn_ref, vp_ref, vc_ref, vn_ref, kx_ref, vx_ref, sink_ref, o_ref = refs
    else:
        q_ref, kx_ref, vx_ref, sink_ref, o_ref = refs
    i = pl.program_id(1)
    blk = ATT_BLOCK
    group = ATT_HEADS // ATT_KV_HEADS
    lane = lax.broadcasted_iota(jnp.int32, (blk, 128), 1)
    lo = lane < ATT_HEAD_DIM
    n_ctx = kx_ref.shape[1]
    row_head = lax.broadcasted_iota(jnp.int32, (group * blk, 1), 0) // blk
    if window:
        r = lax.broadcasted_iota(jnp.int32, (group * blk, 3 * blk + n_ctx), 0) & (blk - 1)
        c = lax.broadcasted_iota(jnp.int32, (group * blk, 3 * blk + n_ctx), 1)
        inner = ((c >= blk) | (c >= r)) & ((c < 2 * blk) | (c >= 3 * blk) | (c - 2 * blk <= r))
        masks = []
        for sb in range(qb):
            m = inner
            if sb == 0:
                m = m & (c >= jnp.where(i > 0, 0, blk))
            if sb == qb - 1:
                m = m & ((c < 2 * blk) | (c >= jnp.where(i < n_steps - 1, 2 * blk, 3 * blk)))
            masks.append(m)
    chains = [(sb, j) for sb in range(qb) for j in range(ATT_KV_HEADS)]
    v_alls, scores, sinks = [], [], []
    for sb, j in chains:
        cols = slice(128 * j, 128 * (j + 1))
        if window:
            k_win = jnp.concatenate([kp_ref[0, :, cols], kc_ref[0, :, cols], kn_ref[0, :, cols]], axis=0)
            v_win = jnp.concatenate([vp_ref[0, :, cols], vc_ref[0, :, cols], vn_ref[0, :, cols]], axis=0)
            rows = slice(sb * blk, (sb + 3) * blk)
            k_all = jnp.concatenate([k_win[rows], kx_ref[0, :, cols]], axis=0)
            v_all = jnp.concatenate([v_win[rows], vx_ref[0, :, cols]], axis=0)
        else:
            k_all, v_all = kx_ref[0, :, cols], vx_ref[0, :, cols]
        v_alls.append(v_all.astype(BF16))
        q_rows = []
        sk = jnp.zeros((group * blk, 1), F32)
        for e in range(group):
            head = group * j + e
            qp = q_ref[0, sb * blk:(sb + 1) * blk, 128 * (head // 2):128 * (head // 2 + 1)]
            q_rows.append(jnp.where(lo if head % 2 == 0 else ~lo, qp, 0.0).astype(BF16))
            sk = jnp.where(row_head == e, sink_ref[head], sk)
        sinks.append(sk)
        s = _dot_nt(jnp.concatenate(q_rows, axis=0), k_all.astype(BF16))
        scores.append(jnp.where(masks[sb], s, NEG_BIG) if window else s)
    mxs = [jnp.maximum(jnp.max(s, axis=-1, keepdims=True), sk) for s, sk in zip(scores, sinks)]
    ps = [jnp.exp(s - mx) for s, mx in zip(scores, mxs)]
    dens = [jnp.sum(p, axis=-1, keepdims=True) + jnp.exp(sk - mx) for p, sk, mx in zip(ps, sinks, mxs)]
    outs = [_dot(p.astype(BF16), v_all) / den for p, v_all, den in zip(ps, v_alls, dens)]
    for (sb, j), o in zip(chains, outs):
        for pair in range(group // 2):
            slab = (group * j) // 2 + pair
            o_ref[0, sb * blk:(sb + 1) * blk, 128 * slab:128 * (slab + 1)] = jnp.where(
                lo, o[2 * pair * blk:(2 * pair + 1) * blk], o[(2 * pair + 1) * blk:(2 * pair + 2) * blk])


def _attention(q, k, v, kx, vx, sink, *, window):
    b_, t, _ = q.shape
    qb = ATT_QUERY_BLOCKS
    tile = qb * ATT_BLOCK
    n_steps = t // tile
    n_blk = t // ATT_BLOCK
    n_ctx = kx.shape[1]
    cur = lambda b, i: (b, i, 0)
    prev = lambda b, i: (b, jnp.maximum(i * qb - 1, 0), 0)
    nxt = lambda b, i: (b, jnp.minimum((i + 1) * qb, n_blk - 1), 0)
    kv_w = kx.shape[-1]
    ctx_spec = pl.BlockSpec((1, n_ctx, kv_w), lambda b, i: (b, 0, 0))
    in_specs = [pl.BlockSpec((1, tile, ATT_DIM), cur)]
    args = [q]
    if window:
        in_specs += [pl.BlockSpec((1, ATT_BLOCK, kv_w), prev), pl.BlockSpec((1, tile, kv_w), cur),
                     pl.BlockSpec((1, ATT_BLOCK, kv_w), nxt)] * 2
        args += [k, k, k, v, v, v]
    in_specs += [ctx_spec, ctx_spec, pl.BlockSpec(memory_space=pltpu.SMEM)]
    args += [kx, vx, sink]
    return pl.pallas_call(
        functools.partial(_att_kernel, n_steps=n_steps, qb=qb, window=window),
        out_shape=jax.ShapeDtypeStruct((b_, t, ATT_DIM), F32),
        grid=(b_, n_steps),
        in_specs=in_specs,
        out_specs=pl.BlockSpec((1, tile, ATT_DIM), cur),
        compiler_params=_cparams("parallel", "parallel"),
        name="window_attn" if window else "ctx_attn",
    )(*args)


def _merge_kernel(x_ref, mod_ref, yc_ref, yl_ref, yg_ref, ya_ref, wgate_ref, pc_ref, pl_ref, pg_ref, pa_ref, wout_ref,
                  g_ref, b_ref, o_ref):
    x = x_ref[0]
    m = mod_ref[0]
    h = (_ln(x) * (1.0 + m[4:5]) + m[3:4]).astype(BF16)
    mrg = None
    for n, (y_ref, p_ref) in enumerate(((yc_ref, pc_ref), (yl_ref, pl_ref), (yg_ref, pg_ref), (ya_ref, pa_ref))):
        gate = jax.nn.sigmoid(_dot(h, wgate_ref[:, n * D_MODEL:(n + 1) * D_MODEL]))
        term = gate * _dot(y_ref[0].astype(BF16), p_ref[...])
        mrg = term if mrg is None else mrg + term
    out = _dot(mrg.astype(BF16), wout_ref[...])
    s = DEEPNORM_ALPHA * x + m[5:6] * out
    o_ref[0] = _ln(s) * g_ref[...] + b_ref[...]


def _merge(x, mod3, ys, wgate, projs, wout, ng, nb, *, ctx, tm):
    b_, t, d = x.shape
    weights = [wgate, *projs, wout, ng, nb]
    return pl.pallas_call(
        _merge_kernel,
        out_shape=jax.ShapeDtypeStruct(x.shape, F32),
        grid=(b_, t // tm),
        in_specs=[_tok_spec(tm, d), _mod_spec(ctx)] + [_tok_spec(tm, y.shape[-1]) for y in ys]
                 + [_resident(w.shape) for w in weights],
        out_specs=_tok_spec(tm, d),
        compiler_params=_cparams("parallel", "parallel"),
        name="merge_ctx" if ctx else "merge",
    )(x, mod3, *ys, *weights)


def _block_diag_halves(w):
    n, bd, _ = w.shape
    per = n // 2
    on_diag = jnp.eye(per, dtype=jnp.bool_)[None, :, None, :, None]
    out = jnp.where(on_diag, w.reshape(2, per, bd, 1, bd), 0.0)
    return out.reshape(2, per * bd, per * bd).astype(BF16)


def _rope_tables(t):
    pos = jnp.arange(t, dtype=jnp.int32)
    row = (pos // GRID_W).astype(F32)
    col = (pos % GRID_W).astype(F32)
    n_freq = ATT_HEAD_DIM // 4
    inv = ROPE_BASE ** (-jnp.arange(n_freq, dtype=F32) / n_freq)
    ang = jnp.concatenate([row[:, None] * inv, col[:, None] * inv], axis=-1)
    cos, sin = jnp.cos(ang), jnp.sin(ang)
    cos128 = jnp.tile(cos, (1, 4))
    sin128 = jnp.tile(jnp.concatenate([-sin, sin], axis=-1), (1, 2))
    return cos128, sin128


def _layer_params(l, ffn_w_in, ffn_w_out, w_in, lru_w_r, lru_w_i, proj_conv, proj_lru, proj_gdn, proj_att, w_out):
    ffn = [(ffn_w_in[l, f].astype(BF16), ffn_w_out[l, f].astype(BF16)) for f in range(2)]
    w = w_in[l]
    kd = ATT_HEAD_DIM
    wk = w[:, _OFF[9]:_OFF[10]]
    wv = w[:, _OFF[10]:_OFF[11]]
    dup = lambda m: jnp.concatenate([m[:, :kd], m[:, :kd], m[:, kd:], m[:, kd:]], axis=1)
    gab = jnp.pad(w[:, _OFF[6]:_OFF[8]], ((0, 0), (0, 128 - 4 * GDN_HEADS)))
    w_cat = jnp.concatenate([w[:, _OFF[0]:_OFF[2]], w[:, _OFF[2]:_OFF[4]], w[:, _OFF[4]:_OFF[6]], gab,
                             w[:, _OFF[8]:_OFF[9]], dup(wk), dup(wv)], axis=1).astype(BF16)
    wgate = w[:, _OFF[11]:_OFF[12]].astype(BF16)
    projs = [p[l].astype(BF16) for p in (proj_conv, proj_lru, proj_gdn, proj_att)]
    lru_w = [(_block_diag_halves(lru_w_r[l, d]), _block_diag_halves(lru_w_i[l, d])) for d in range(2)]
    return dict(ffn=ffn, w_cat=w_cat, wgate=wgate, projs=projs, wout=w_out[l].astype(BF16), lru_w=lru_w)


def kernel(x, c, ctx, c_ctx, ada_w, ada_b, norm_g, norm_b, ffn_w_in, ffn_w_out, w_in, conv_w, conv_b, conv_norm_g, conv_norm_b, lru_conv_w, lru_conv_b, lru_w_r, lru_b_r, lru_w_i, lru_b_i, lru_lam, gdn_conv_w, gdn_a_log, gdn_dt_bias, gdn_norm_g, att_sink, proj_conv, proj_lru, proj_gdn, proj_att, w_out):
    b_, t, d = x.shape
    lc = ctx.shape[1]
    assert d == D_MODEL and t % FFN_TILE == 0 and lc == 256 and b_ <= CTX_MOD_ROW
    c_all = jnp.zeros((MOD_ROWS, d), F32).at[:b_].set(c).at[CTX_MOD_ROW].set(c_ctx)
    mod_all = _modulation(c_all, ada_w, ada_b)
    rope_tabs = _rope_tables(t)
    row2 = lambda v: v.reshape(1, -1)
    xc = ctx
    for l in range(DEPTH):
        last = l == DEPTH - 1
        p = _layer_params(l, ffn_w_in, ffn_w_out, w_in, lru_w_r, lru_w_i, proj_conv, proj_lru, proj_gdn, proj_att, w_out)
        mod3 = mod_all[l]
        ng = [row2(norm_g[l, j]) for j in range(3)]
        nb = [row2(norm_b[l, j]) for j in range(3)]

        x = _ffn(x, mod3, *p["ffn"][0], ng[0], nb[0], j=0, ctx=False, tm=FFN_TILE)
        xc = _ffn(xc, mod3, *p["ffn"][0], ng[0], nb[0], j=0, ctx=True, tm=lc)

        z = _inproj(x, mod3, p["w_cat"], rope_tabs, ctx=False, tm=PROJ_TILE)
        zc = _inproj(xc, mod3, p["w_cat"], None, ctx=True, tm=lc)
        conv_in, lru_in, gdn_in, gab, q, k, v = z
        conv_c, lru_c, gdn_c, gab_c, q_c, k_c, v_c = zc

        conv_prm = (conv_w[l], row2(conv_b[l]), row2(conv_norm_g[l]), row2(conv_norm_b[l]))
        y_conv = _conformer(conv_in, *conv_prm, tt=SEQ_TILE)

        lru_prm = [(lru_conv_w[l, dd], row2(lru_conv_b[l, dd]), p["lru_w"][dd][0], row2(lru_b_r[l, dd]),
                    p["lru_w"][dd][1], row2(lru_b_i[l, dd]), row2(lru_lam[l, dd])) for dd in range(2)]
        h0 = jnp.zeros((b_, 1, LRU_DIM), F32)
        hc_f, st_f = _lru(lru_c, h0, lru_prm[0], None, tt=lc, reverse=False)
        hl_f, _ = _lru(lru_in, st_f, lru_prm[0], None, tt=SEQ_TILE, reverse=False)
        y_lru_c, st_r = _lru(lru_c, h0, lru_prm[1], None if last else hc_f, tt=lc, reverse=True)
        y_lru, _ = _lru(lru_in, st_r, lru_prm[1], hl_f, tt=SEQ_TILE, reverse=True)

        pad8 = lambda a: jnp.pad(a.reshape(1, -1), ((0, 0), (0, 128 - 2 * GDN_HEADS)))
        alog_row, dtb_row = pad8(gdn_a_log[l]), pad8(gdn_dt_bias[l])
        gdn_prm = [(gdn_conv_w[l, dd], alog_row, dtb_row) for dd in range(2)]
        gn = row2(gdn_norm_g[l])
        s0 = jnp.zeros((b_, GDN_HEADS, GDN_HEAD_DIM, GDN_HEAD_DIM), F32)
        oc_f, sg_f = _gdn(gdn_c, gab_c, s0, gdn_prm[0], None, None, tt=lc, direction=0)
        ol_f, _ = _gdn(gdn_in, gab, sg_f, gdn_prm[0], None, None, tt=GDN_TILE, direction=0)
        y_gdn_c, sg_r = _gdn(gdn_c, gab_c, s0, gdn_prm[1], None if last else oc_f, gn, tt=lc, direction=1)
        y_gdn, _ = _gdn(gdn_in, gab, sg_r, gdn_prm[1], ol_f, gn, tt=GDN_TILE, direction=1)

        y_att = _attention(q, k, v, k_c, v_c, att_sink[l], window=True)

        x = _merge(x, mod3, (y_conv, y_lru, y_gdn, y_att), p["wgate"], p["projs"], p["wout"], ng[1], nb[1],
                   ctx=False, tm=PROJ_TILE)
        x = _ffn(x, mod3, *p["ffn"][1], ng[2], nb[2], j=2, ctx=False, tm=FFN_TILE)
        if not last:
            y_conv_c = _conformer(conv_c, *conv_prm, tt=lc)
            y_att_c = _attention(q_c, None, None, k_c, v_c, att_sink[l], window=False)
            xc = _merge(xc, mod3, (y_conv_c, y_lru_c, y_gdn_c, y_att_c), p["wgate"], p["projs"], p["wout"],
                        ng[1], nb[1], ctx=True, tm=lc)
            xc = _ffn(xc, mod3, *p["ffn"][1], ng[2], nb[2], j=2, ctx=True, tm=lc)
    return x
```

```python
import functools

import numpy as np
import jax
import jax.numpy as jnp
from jax import lax
from jax.experimental import pallas as pl
from jax.experimental.pallas import tpu as pltpu

F32 = jnp.float32
BF16 = jnp.bfloat16

D_MODEL = 1024
DEPTH = 2
GRID_W = 64
CONV_DIM = 512
CONV_WIDTH = 31
LRU_DIM = 512
LRU_BLOCKS = 8
LRU_CONV = 4
LRU_C = 8.0
GDN_HEADS = 4
GDN_HEAD_DIM = 128
GDN_DIM = GDN_HEADS * GDN_HEAD_DIM
GDN_CONV = 4
GDN_CHUNK = 64
ATT_HEADS = 8
ATT_KV_HEADS = 2
ATT_HEAD_DIM = 64
ATT_DIM = ATT_HEADS * ATT_HEAD_DIM
ATT_BLOCK = 128
ROPE_BASE = 10000.0
N_BRANCH = 4
D_FF = 2816
LN_EPS = 1e-5
NORM_EPS = 1e-6
DEEPNORM_ALPHA = (2 * DEPTH) ** 0.25

CTX_MOD_ROW = 4
MOD_ROWS = 8
FF_CHUNKS = 11
SUBLANES = 8
CONV_HALO = 16
VMEM_LIMIT = 56 * 1024 * 1024
FFN_GROUP = 512
FFN_TILE = 2 * FFN_GROUP
PROJ_TILE = 512
SEQ_TILE = 512
GDN_TILE = 256
GDN_STAGE_RATES = (3, 2, 1)
ATT_QUERY_BLOCKS = 4
LRU_TILE = 1024
NEG_BIG = -1e30

_OFF = np.cumsum([0, CONV_DIM, CONV_DIM, LRU_DIM, LRU_DIM, 3 * GDN_DIM, GDN_DIM, 2 * GDN_HEADS, 2 * GDN_HEADS,
                  ATT_DIM, ATT_KV_HEADS * ATT_HEAD_DIM, ATT_KV_HEADS * ATT_HEAD_DIM, N_BRANCH * D_MODEL]).tolist()


def _cparams(*sem):
    return pltpu.CompilerParams(dimension_semantics=sem, vmem_limit_bytes=VMEM_LIMIT)


def _resident(shape):
    nd = len(shape)
    return pl.BlockSpec(tuple(shape), lambda *_: (0,) * nd, pipeline_mode=pl.Buffered(1))


def _ln(x):
    mu = jnp.mean(x, axis=-1, keepdims=True)
    xc = x - mu
    var = jnp.mean(xc * xc, axis=-1, keepdims=True)
    return xc * lax.rsqrt(var + LN_EPS)


def _silu(x):
    return x * jax.nn.sigmoid(x)


def _softplus(x):
    return jnp.maximum(x, 0.0) + jnp.log1p(jnp.exp(-jnp.abs(x)))


def _dot(a, b):
    return jnp.dot(a, b, preferred_element_type=F32)


def _dot_nt(a, b):
    return lax.dot_general(a, b, (((1,), (1,)), ((), ())), preferred_element_type=F32)


def _dot_tn(a, b):
    return lax.dot_general(a, b, (((0,), (0,)), ((), ())), preferred_element_type=F32)


def _split(a):
    hi = a.astype(BF16)
    return hi, (a - hi.astype(F32)).astype(BF16)


def _mod_spec(ctx):
    if ctx:
        return pl.BlockSpec((1, 9, D_MODEL), lambda b, i: (CTX_MOD_ROW, 0, 0))
    return pl.BlockSpec((1, 9, D_MODEL), lambda b, i: (b, 0, 0))


def _tok_spec(tm, width):
    return pl.BlockSpec((1, tm, width), lambda b, i: (b, i, 0))


def _mod_kernel(c_ref, w_ref, b_ref, o_ref):
    s = _silu(c_ref[...]).astype(BF16)
    o_ref[0] = _dot(s, w_ref[0].astype(BF16)) + b_ref[0]


def _modulation(c_all, ada_w, ada_b):
    n_l, d, n = ada_w.shape
    tn = 1024
    out = pl.pallas_call(
        _mod_kernel,
        out_shape=jax.ShapeDtypeStruct((n_l, MOD_ROWS, n), F32),
        grid=(n_l, n // tn),
        in_specs=[pl.BlockSpec((MOD_ROWS, d), lambda l, j: (0, 0)),
                  pl.BlockSpec((1, d, tn), lambda l, j: (l, 0, j)),
                  pl.BlockSpec((1, 1, tn), lambda l, j: (l, 0, j))],
        out_specs=pl.BlockSpec((1, MOD_ROWS, tn), lambda l, j: (l, 0, j)),
        compiler_params=_cparams("arbitrary", "arbitrary"),
        name="adaln_mod",
    )(c_all, ada_w, ada_b.reshape(n_l, 1, n))
    return out.reshape(n_l, MOD_ROWS, 9, d)


def _ffn_kernel(x_ref, mod_ref, wgu_ref, wd_ref, g_ref, b_ref, o_ref, *, j, n_sub):
    m = mod_ref[0]
    shift, scale, gate = m[3 * j:3 * j + 1], m[3 * j + 1:3 * j + 2], m[3 * j + 2:3 * j + 3]
    rows = x_ref.shape[1] // n_sub
    fc = D_FF // FF_CHUNKS
    for r in range(n_sub):
        rs = slice(r * rows, (r + 1) * rows)
        x = x_ref[0, rs]
        h = (_ln(x) * (1.0 + scale) + shift).astype(BF16)
        acc = None
        for f in range(FF_CHUNKS):
            g = _dot(h, wgu_ref[:, f * fc:(f + 1) * fc])
            u = _dot(h, wgu_ref[:, D_FF + f * fc:D_FF + (f + 1) * fc])
            a = (_silu(g) * u).astype(BF16)
            y = _dot(a, wd_ref[f * fc:(f + 1) * fc, :])
            acc = y if acc is None else acc + y
        s = DEEPNORM_ALPHA * x + (0.5 * gate) * acc
        o_ref[0, rs] = _ln(s) * g_ref[...] + b_ref[...]


def _ffn(x, mod3, wgu, wd, ng, nb, *, j, ctx, tm):
    b_, t, d = x.shape
    return pl.pallas_call(
        functools.partial(_ffn_kernel, j=j, n_sub=max(tm // FFN_GROUP, 1)),
        out_shape=jax.ShapeDtypeStruct(x.shape, F32),
        grid=(b_, t // tm),
        in_specs=[_tok_spec(tm, d), _mod_spec(ctx), _resident(wgu.shape), _resident(wd.shape),
                  _resident(ng.shape), _resident(nb.shape)],
        out_specs=_tok_spec(tm, d),
        compiler_params=_cparams("parallel", "parallel"),
        name="ffn_ctx" if ctx else "ffn",
    )(x, mod3, wgu, wd, ng, nb)


def _rope128(t, cos, sin_signed, lane_lo):
    swapped = jnp.where(lane_lo, pltpu.roll(t, 96, 1), pltpu.roll(t, 32, 1))
    return t * cos + swapped * sin_signed


def _inproj_kernel(*refs, rope):
    if rope:
        x_ref, mod_ref, w_ref, cos_ref, sin_ref, o_conv, o_lru, o_gdn, o_gab, o_q, o_k, o_v = refs
    else:
        x_ref, mod_ref, w_ref, o_conv, o_lru, o_gdn, o_gab, o_q, o_k, o_v = refs
    m = mod_ref[0]
    h = (_ln(x_ref[0]) * (1.0 + m[4:5]) + m[3:4]).astype(BF16)
    o_conv[0] = _dot(h, w_ref[:, 0:1024])
    o_lru[0] = _dot(h, w_ref[:, 1024:2048])
    o_gdn[0] = _dot(h, w_ref[:, 2048:4096])
    o_gab[0] = _dot(h, w_ref[:, 4096:4224])
    q = _dot(h, w_ref[:, 4224:4736])
    k = _dot(h, w_ref[:, 4736:4992])
    o_v[0] = _dot(h, w_ref[:, 4992:5248])
    scale = ATT_HEAD_DIM ** -0.5
    if rope:
        cos, sin = cos_ref[...], sin_ref[...]
        lane_lo = (lax.broadcasted_iota(jnp.int32, cos.shape, 1) % ATT_HEAD_DIM) < (ATT_HEAD_DIM // 2)
        for s in range(ATT_DIM // 128):
            o_q[0, :, 128 * s:128 * (s + 1)] = _rope128(q[:, 128 * s:128 * (s + 1)], cos, sin, lane_lo) * scale
        for s in range(2):
            o_k[0, :, 128 * s:128 * (s + 1)] = _rope128(k[:, 128 * s:128 * (s + 1)], cos, sin, lane_lo)
    else:
        o_q[0] = q * scale
        o_k[0] = k


_INPROJ_WIDTHS = (1024, 1024, 2048, 128, 512, 256, 256)


def _inproj(x, mod3, w_cat, rope_tabs, *, ctx, tm):
    b_, t, d = x.shape
    in_specs = [_tok_spec(tm, d), _mod_spec(ctx), _resident(w_cat.shape)]
    args = [x, mod3, w_cat]
    if rope_tabs is not None:
        in_specs += [pl.BlockSpec((tm, 128), lambda b, i: (i, 0))] * 2
        args += list(rope_tabs)
    return pl.pallas_call(
        functools.partial(_inproj_kernel, rope=rope_tabs is not None),
        out_shape=[jax.ShapeDtypeStruct((b_, t, w), F32) for w in _INPROJ_WIDTHS],
        grid=(b_, t // tm),
        in_specs=in_specs,
        out_specs=[_tok_spec(tm, w) for w in _INPROJ_WIDTHS],
        compiler_params=_cparams("parallel", "parallel"),
        name="inproj_ctx" if ctx else "inproj",
    )(*args)


def _conv_kernel(main_ref, prev_ref, next_ref, w_ref, cb_ref, g_ref, b_ref, o_ref, ubuf, *, tt, n_t, rb):
    i = pl.program_id(1)

    def glu(blk):
        return blk[:, :CONV_DIM] * jax.nn.sigmoid(blk[:, CONV_DIM:])

    n_rows = tt + 2 * CONV_HALO
    ubuf[0, 0:CONV_HALO] = jnp.where(i > 0, glu(prev_ref[0]), 0.0)
    ubuf[0, CONV_HALO:CONV_HALO + tt] = glu(main_ref[0])
    ubuf[0, CONV_HALO + tt:] = jnp.where(i < n_t - 1, glu(next_ref[0]), 0.0)
    for s in range(1, SUBLANES):
        ubuf[s, 0:n_rows - SUBLANES] = ubuf[0, s:s + n_rows - SUBLANES]
    base = CONV_HALO - (CONV_WIDTH - 1) // 2
    for r0 in range(0, tt, rb):
        acc = None
        for k in range(CONV_WIDTH):
            tile, s = divmod(base + k, SUBLANES)
            start = r0 + tile * SUBLANES
            term = ubuf[s, start:start + rb, :] * w_ref[k]
            acc = term if acc is None else acc + term
        y = _ln(acc + cb_ref[...]) * g_ref[...] + b_ref[...]
        o_ref[0, r0:r0 + rb, :] = _silu(y)


def _conformer(conv_in, w, cb, ng, nb, *, tt):
    b_, t, width = conv_in.shape
    n_t = t // tt
    hb = tt // CONV_HALO
    n_h = t // CONV_HALO
    rb = 2 * SUBLANES
    w = jnp.broadcast_to(w[:, None, :], (CONV_WIDTH, rb, CONV_DIM))
    return pl.pallas_call(
        functools.partial(_conv_kernel, tt=tt, n_t=n_t, rb=rb),
        out_shape=jax.ShapeDtypeStruct((b_, t, CONV_DIM), F32),
        grid=(b_, n_t),
        in_specs=[_tok_spec(tt, width),
                  pl.BlockSpec((1, CONV_HALO, width), lambda b, i: (b, jnp.maximum(i * hb - 1, 0), 0)),
                  pl.BlockSpec((1, CONV_HALO, width), lambda b, i: (b, jnp.minimum((i + 1) * hb, n_h - 1), 0)),
                  _resident(w.shape), _resident(cb.shape), _resident(ng.shape), _resident(nb.shape)],
        out_specs=_tok_spec(tt, CONV_DIM),
        scratch_shapes=[pltpu.VMEM((SUBLANES, tt + 2 * CONV_HALO, CONV_DIM), F32)],
        compiler_params=_cparams("parallel", "parallel"),
        name="conformer_conv",
    )(conv_in, conv_in, conv_in, w, cb, ng, nb)


def _halo_specs(tt, t, width, reverse):
    n_t = t // tt
    hb = tt // SUBLANES
    n_h = t // SUBLANES
    if reverse:
        main = pl.BlockSpec((1, tt, width), lambda b, i: (b, n_t - 1 - i, 0))
        halo = pl.BlockSpec((1, SUBLANES, width), lambda b, i: (b, jnp.minimum((n_t - i) * hb, n_h - 1), 0))
    else:
        main = pl.BlockSpec((1, tt, width), lambda b, i: (b, i, 0))
        halo = pl.BlockSpec((1, SUBLANES, width), lambda b, i: (b, jnp.maximum(i * hb - 1, 0), 0))
    return main, halo


def _fill_conv_buffer(xbuf, main, halo, first, tt, reverse):
    halo = jnp.where(first, 0.0, halo)
    if reverse:
        xbuf[0:tt] = main
        xbuf[tt:] = halo
        return lambda k: LRU_CONV - 1 - k
    xbuf[0:SUBLANES] = halo
    xbuf[SUBLANES:] = main
    return lambda k: SUBLANES - (LRU_CONV - 1) + k


def _lru_kernel(*refs, tt, reverse, final):
    if final:
        (main_ref, halo_ref, h0_ref, cw_ref, cb_ref, wr_ref, br_ref, wi_ref, bi_ref, lam_ref, hf_ref,
         o_ref, st_ref, xbuf, abuf, bbuf, carry) = refs
    else:
        (main_ref, halo_ref, h0_ref, cw_ref, cb_ref, wr_ref, br_ref, wi_ref, bi_ref, lam_ref,
         o_ref, st_ref, xbuf, abuf, bbuf, carry) = refs
    i = pl.program_id(1)

    @pl.when(i == 0)
    def _():
        carry[...] = h0_ref[0]

    off = _fill_conv_buffer(xbuf, main_ref[0, :, 0:LRU_DIM], halo_ref[0, :, 0:LRU_DIM], i == 0, tt, reverse)
    xc = cb_ref[...]
    for k in range(LRU_CONV):
        xc = xc + xbuf[off(k):off(k) + tt, :] * cw_ref[k:k + 1, :]
    xh = xc.astype(BF16)
    half = LRU_DIM // 2
    r_lin = jnp.concatenate([_dot(xh[:, :half], wr_ref[0]), _dot(xh[:, half:], wr_ref[1])], axis=1)
    i_lin = jnp.concatenate([_dot(xh[:, :half], wi_ref[0]), _dot(xh[:, half:], wi_ref[1])], axis=1)
    r = jax.nn.sigmoid(r_lin + br_ref[...])
    gi = jax.nn.sigmoid(i_lin + bi_ref[...])
    log_a = (-LRU_C) * r * _softplus(-lam_ref[...])
    a = jnp.exp(log_a)
    th = jnp.tanh(log_a)
    bb = jnp.sqrt(-2.0 * th / (1.0 - th)) * (gi * xc)

    n_g = tt // SUBLANES
    a3 = a.reshape(n_g, SUBLANES, LRU_DIM)
    b3 = bb.reshape(n_g, SUBLANES, LRU_DIM)
    row = lax.broadcasted_iota(jnp.int32, a3.shape, 1)
    for s in (1, 2, 4):
        if reverse:
            keep = row < SUBLANES - s
            a_sh = pltpu.roll(a3, SUBLANES - s, 1)
            b_sh = pltpu.roll(b3, SUBLANES - s, 1)
        else:
            keep = row >= s
            a_sh = pltpu.roll(a3, s, 1)
            b_sh = pltpu.roll(b3, s, 1)
        b3 = jnp.where(keep, a3 * b_sh + b3, b3)
        a3 = jnp.where(keep, a3 * a_sh, a3)
    abuf[...] = a3.reshape(tt, LRU_DIM)
    bbuf[...] = b3.reshape(tt, LRU_DIM)

    hprev = carry[...]
    order = range(n_g - 1, -1, -1) if reverse else range(n_g)
    edge = 0 if reverse else SUBLANES - 1
    for g in order:
        rows = slice(g * SUBLANES, (g + 1) * SUBLANES)
        hg = abuf[rows, :] * hprev + bbuf[rows, :]
        hprev = hg[edge:edge + 1, :]
        if final:
            gate = jax.nn.gelu(main_ref[0, rows, LRU_DIM:2 * LRU_DIM])
            o_ref[0, rows, :] = (hf_ref[0, rows, :] + hg) * gate
        else:
            o_ref[0, rows, :] = hg
    carry[...] = hprev
    st_ref[0] = hprev


def _lru(lru_in, h0, prm, h_fwd, *, tt, reverse):
    b_, t, width = lru_in.shape
    final = h_fwd is not None
    main, halo = _halo_specs(tt, t, width, reverse)
    in_specs = [main, halo, pl.BlockSpec((1, 1, LRU_DIM), lambda b, i: (b, 0, 0))] + [_resident(p.shape) for p in prm]
    args = [lru_in, lru_in, h0] + list(prm)
    out_tile = pl.BlockSpec((1, tt, LRU_DIM), main.index_map)
    if final:
        in_specs.append(out_tile)
        args.append(h_fwd)
    return pl.pallas_call(
        functools.partial(_lru_kernel, tt=tt, reverse=reverse, final=final),
        out_shape=[jax.ShapeDtypeStruct((b_, t, LRU_DIM), F32), jax.ShapeDtypeStruct((b_, 1, LRU_DIM), F32)],
        grid=(b_, t // tt),
        in_specs=in_specs,
        out_specs=[out_tile, pl.BlockSpec((1, 1, LRU_DIM), lambda b, i: (b, 0, 0))],
        scratch_shapes=[pltpu.VMEM((tt + SUBLANES, LRU_DIM), F32), pltpu.VMEM((tt, LRU_DIM), F32),
                        pltpu.VMEM((tt, LRU_DIM), F32), pltpu.VMEM((1, LRU_DIM), F32)],
        compiler_params=_cparams("parallel", "arbitrary"),
        name="rglru_rev" if reverse else "rglru_fwd",
    )(*args)


def _interleave(*streams):
    live = list(streams)
    while live:
        still = []
        for gen, rate in live:
            for _ in range(rate):
                if next(gen, StopIteration) is StopIteration:
                    break
            else:
                still.append((gen, rate))
        live = still


def _unit_tri_inverses(ms, ri, ci, eye, out):
    def same_block(log2_size):
        return jnp.right_shift(ri, log2_size) == jnp.right_shift(ci, log2_size)

    base = same_block(3)
    ps = [jnp.where(base, -m, 0.0) for m in ms]
    xs = [eye + p for p in ps]
    for _ in range(2):
        p16 = [p.astype(BF16) for p in ps]
        ps = [_dot(p, p) for p in p16]
        yield
        xs = [x + _dot(x.astype(BF16), p.astype(BF16)) for x, p in zip(xs, ps)]
        yield
    for log2_size in (3, 4, 5):
        level = same_block(log2_size + 1) & ~same_block(log2_size)
        x16 = [x.astype(BF16) for x in xs]
        ys = [_dot(x, jnp.where(level, m, 0.0).astype(BF16)) for x, m in zip(x16, ms)]
        yield
        xs = [x - _dot(y.astype(BF16), xb) for x, y, xb in zip(xs, ys, x16)]
        yield
    out.extend(xs)


def _gdn_kernel(*refs, tt, nb, direction, final):
    if final:
        (main_ref, halo_ref, gab_ref, s0_ref, cw_ref, alog_ref, dtb_ref, of_ref, gn_ref,
         o_ref, sfin_ref, xbuf, s_ref) = refs
    else:
        (main_ref, halo_ref, gab_ref, s0_ref, cw_ref, alog_ref, dtb_ref,
         o_ref, sfin_ref, xbuf, s_ref) = refs
    reverse = direction == 1
    i = pl.program_id(0)
    c_ = GDN_CHUNK
    hd = GDN_HEAD_DIM
    qkv_w = 3 * GDN_DIM

    @pl.when(i == 0)
    def _():
        s_ref[...] = s0_ref[...]

    for b in range(nb):
        off = _fill_conv_buffer(xbuf.at[b], main_ref[b, :, 0:qkv_w], halo_ref[b, :, 0:qkv_w], i == 0, tt, reverse)

    ri = lax.broadcasted_iota(jnp.int32, (c_, c_), 0)
    ci = lax.broadcasted_iota(jnp.int32, (c_, c_), 1)
    incl = (ci >= ri) if reverse else (ci <= ri)
    strict = (ci > ri) if reverse else (ci < ri)
    eye = (ri == ci).astype(F32)
    ones_tri = jnp.where(incl, 1.0, 0.0).astype(BF16)
    last = 0 if reverse else c_ - 1

    n_c = tt // c_
    chunks = list(range(n_c - 1, -1, -1) if reverse else range(n_c))
    heads = range(GDN_HEADS)

    batch = range(nb)
    lanes = [(b, h) for b in batch for h in heads]

    def elementwise_stage(c, pre):
        r0 = c * c_
        for b in batch:
            acc = None
            for k in range(GDN_CONV):
                term = xbuf[b, r0 + off(k):r0 + off(k) + c_, :] * cw_ref[k:k + 1, :]
                acc = term if acc is None else acc + term
            u = _silu(acc)
            yield
            ga = gab_ref[b, r0:r0 + c_, :]
            g_all = -jnp.exp(alog_ref[...]) * _softplus(ga + dtb_ref[...])
            beta_all = jax.nn.sigmoid(ga)
            g_hi, g_lo = _split(g_all)
            g_lo2 = (g_all - g_hi.astype(F32) - g_lo.astype(F32)).astype(BF16)
            gcum = _dot(ones_tri, g_hi) + (_dot(ones_tri, g_lo) + _dot(ones_tri, g_lo2))
            gcum_t = gcum.T
            g_last = gcum[last:last + 1, :]
            for h in heads:
                col = direction * GDN_HEADS + h
                q, k, v = (u[:, part * GDN_DIM + h * hd:part * GDN_DIM + (h + 1) * hd] for part in range(3))
                q = q * (lax.rsqrt(jnp.sum(q * q, axis=-1, keepdims=True) + NORM_EPS) * (hd ** -0.5))
                k = k * lax.rsqrt(jnp.sum(k * k, axis=-1, keepdims=True) + NORM_EPS)
                gc = gcum[:, col:col + 1]
                gl = g_last[:, col:col + 1]
                beta = beta_all[:, 2 * GDN_HEADS + col:2 * GDN_HEADS + col + 1]
                decay = jnp.exp(jnp.where(incl, gc - gcum_t[col:col + 1, :], -jnp.inf))
                kb = k * beta
                k16 = k.astype(BF16)
                eg = jnp.exp(gc)
                pre[b, h] = dict(
                    m=jnp.where(strict, _dot_nt(kb.astype(BF16), k16) * decay, 0.0),
                    attn=(_dot_nt(q.astype(BF16), k16) * decay).astype(BF16),
                    rhs=jnp.concatenate([v * beta, kb * eg], axis=1).astype(BF16),
                    qd=q * eg, ks=(k * jnp.exp(gl - gc)).astype(BF16), cd=jnp.exp(gl))
                yield

    def solve_stage(pre):
        t_inv = []
        yield from _unit_tri_inverses([pre[bh]["m"] for bh in lanes], ri, ci, eye, t_inv)
        for bh, t in zip(lanes, t_inv):
            sol = _dot(t.astype(BF16), pre[bh]["rhs"])
            pre[bh]["u"] = sol[:, :hd]
            pre[bh]["wq"] = jnp.concatenate([sol[:, hd:], pre[bh]["qd"]], axis=0).astype(BF16)
            if bh[1] == GDN_HEADS - 1:
                yield

    def recurrence_stage(c, cur, state):
        r0 = c * c_
        wq = {bh: _dot(cur[bh]["wq"], state[bh].astype(BF16)) for bh in lanes}
        yield
        vn16 = {bh: (cur[bh]["u"] - wq[bh][:c_]).astype(BF16) for bh in lanes}
        outs = {bh: wq[bh][c_:] + _dot(cur[bh]["attn"], vn16[bh]) for bh in lanes}
        yield
        state.update({bh: state[bh] * cur[bh]["cd"] + _dot_tn(cur[bh]["ks"], vn16[bh]) for bh in lanes})
        yield
        for b, h in lanes:
            cols = slice(h * hd, (h + 1) * hd)
            o = outs[b, h]
            if final:
                o = o + of_ref[b, r0:r0 + c_, cols]
                z = main_ref[b, r0:r0 + c_, qkv_w + h * hd:qkv_w + (h + 1) * hd]
                y = o * lax.rsqrt(jnp.mean(o * o, axis=-1, keepdims=True) + NORM_EPS) * gn_ref[...]
                o_ref[b, r0:r0 + c_, cols] = y * _silu(z)
            else:
                o_ref[b, r0:r0 + c_, cols] = o
            if h == GDN_HEADS - 1:
                yield

    state = {bh: s_ref[bh] for bh in lanes}
    pre = {n: {} for n in range(n_c)}
    for n in range(n_c + 2):
        streams = []
        if n < n_c:
            streams.append((elementwise_stage(chunks[n], pre[n]), GDN_STAGE_RATES[0]))
        if 1 <= n <= n_c:
            streams.append((solve_stage(pre[n - 1]), GDN_STAGE_RATES[1]))
        if n >= 2:
            streams.append((recurrence_stage(chunks[n - 2], pre.pop(n - 2), state), GDN_STAGE_RATES[2]))
        _interleave(*streams)
    for bh in lanes:
        s_ref[bh] = state[bh]
    sfin_ref[...] = s_ref[...]


def _gdn(gdn_in, gab, s0, prm, o_fwd, gn, *, tt, direction):
    b_, t, width = gdn_in.shape
    final = o_fwd is not None
    reverse = direction == 1
    n_t = t // tt
    hb = tt // SUBLANES
    n_h = t // SUBLANES
    if reverse:
        tile = lambda i: (0, n_t - 1 - i, 0)
        halo_map = lambda i: (0, jnp.minimum((n_t - i) * hb, n_h - 1), 0)
    else:
        tile = lambda i: (0, i, 0)
        halo_map = lambda i: (0, jnp.maximum(i * hb - 1, 0), 0)
    st_shape = (b_, GDN_HEADS, GDN_HEAD_DIM, GDN_HEAD_DIM)
    st_spec = pl.BlockSpec(st_shape, lambda i: (0, 0, 0, 0))
    out_tile = pl.BlockSpec((b_, tt, GDN_DIM), tile)
    in_specs = [pl.BlockSpec((b_, tt, width), tile), pl.BlockSpec((b_, SUBLANES, width), halo_map),
                pl.BlockSpec((b_, tt, 128), tile), st_spec] + [_resident(p.shape) for p in prm]
    args = [gdn_in, gdn_in, gab, s0] + list(prm)
    if final:
        in_specs += [out_tile, _resident(gn.shape)]
        args += [o_fwd, gn]
    return pl.pallas_call(
        functools.partial(_gdn_kernel, tt=tt, nb=b_, direction=direction, final=final),
        out_shape=[jax.ShapeDtypeStruct((b_, t, GDN_DIM), F32), jax.ShapeDtypeStruct(st_shape, F32)],
        grid=(n_t,),
        in_specs=in_specs,
        out_specs=[out_tile, st_spec],
        scratch_shapes=[pltpu.VMEM((b_, tt + SUBLANES, 3 * GDN_DIM), F32), pltpu.VMEM(st_shape, F32)],
        compiler_params=_cparams("arbitrary"),
        name="gdn_rev" if reverse else "gdn_fwd",
    )(*args)


def _att_kernel(*refs, n_steps, qb, window):
    if window:
        q_ref, kp_ref, kc_ref, kn_ref, vp_ref, vc_ref, vn_ref, kx_ref, vx_ref, sink_ref, o_ref = refs
    else:
        q_ref, kx_ref, vx_ref, sink_ref, o_ref = refs
    i = pl.program_id(1)
    blk = ATT_BLOCK
    group = ATT_HEADS // ATT_KV_HEADS
    lane = lax.broadcasted_iota(jnp.int32, (blk, 128), 1)
    lo = lane < ATT_HEAD_DIM
    n_ctx = kx_ref.shape[1]
    row_head = lax.broadcasted_iota(jnp.int32, (group * blk, 1), 0) // blk
    if window:
        r = lax.broadcasted_iota(jnp.int32, (group * blk, 3 * blk + n_ctx), 0) & (blk - 1)
        c = lax.broadcasted_iota(jnp.int32, (group * blk, 3 * blk + n_ctx), 1)
        inner = ((c >= blk) | (c >= r)) & ((c < 2 * blk) | (c >= 3 * blk) | (c - 2 * blk <= r))
        masks = []
        for sb in range(qb):
            m = inner
            if sb == 0:
                m = m & (c >= jnp.where(i > 0, 0, blk))
            if sb == qb - 1:
                m = m & ((c < 2 * blk) | (c >= jnp.where(i < n_steps - 1, 2 * blk, 3 * blk)))
            masks.append(m)
    chains = [(sb, j) for sb in range(qb) for j in range(ATT_KV_HEADS)]
    v_alls, scores, sinks = [], [], []
    for sb, j in chains:
        cols = slice(128 * j, 128 * (j + 1))
        if window:
            k_win = jnp.concatenate([kp_ref[0, :, cols], kc_ref[0, :, cols], kn_ref[0, :, cols]], axis=0)
            v_win = jnp.concatenate([vp_ref[0, :, cols], vc_ref[0, :, cols], vn_ref[0, :, cols]], axis=0)
            rows = slice(sb * blk, (sb + 3) * blk)
            k_all = jnp.concatenate([k_win[rows], kx_ref[0, :, cols]], axis=0)
            v_all = jnp.concatenate([v_win[rows], vx_ref[0, :, cols]], axis=0)
        else:
            k_all, v_all = kx_ref[0, :, cols], vx_ref[0, :, cols]
        v_alls.append(v_all.astype(BF16))
        q_rows = []
        sk = jnp.zeros((group * blk, 1), F32)
        for e in range(group):
            head = group * j + e
            qp = q_ref[0, sb * blk:(sb + 1) * blk, 128 * (head // 2):128 * (head // 2 + 1)]
            q_rows.append(jnp.where(lo if head % 2 == 0 else ~lo, qp, 0.0).astype(BF16))
            sk = jnp.where(row_head == e, sink_ref[head], sk)
        sinks.append(sk)
        s = _dot_nt(jnp.concatenate(q_rows, axis=0), k_all.astype(BF16))
        scores.append(jnp.where(masks[sb], s, NEG_BIG) if window else s)
    mxs = [jnp.maximum(jnp.max(s, axis=-1, keepdims=True), sk) for s, sk in zip(scores, sinks)]
    ps = [jnp.exp(s - mx) for s, mx in zip(scores, mxs)]
    dens = [jnp.sum(p, axis=-1, keepdims=True) + jnp.exp(sk - mx) for p, sk, mx in zip(ps, sinks, mxs)]
    outs = [_dot(p.astype(BF16), v_all) / den for p, v_all, den in zip(ps, v_alls, dens)]
    for (sb, j), o in zip(chains, outs):
        for pair in range(group // 2):
            slab = (group * j) // 2 + pair
            o_ref[0, sb * blk:(sb + 1) * blk, 128 * slab:128 * (slab + 1)] = jnp.where(
                lo, o[2 * pair * blk:(2 * pair + 1) * blk], o[(2 * pair + 1) * blk:(2 * pair + 2) * blk])


def _attention(q, k, v, kx, vx, sink, *, window):
    b_, t, _ = q.shape
    qb = min(ATT_QUERY_BLOCKS, t // ATT_BLOCK)
    tile = qb * ATT_BLOCK
    n_steps = t // tile
    n_blk = t // ATT_BLOCK
    n_ctx = kx.shape[1]
    cur = lambda b, i: (b, i, 0)
    prev = lambda b, i: (b, jnp.maximum(i * qb - 1, 0), 0)
    nxt = lambda b, i: (b, jnp.minimum((i + 1) * qb, n_blk - 1), 0)
    kv_w = kx.shape[-1]
    ctx_spec = pl.BlockSpec((1, n_ctx, kv_w), lambda b, i: (b, 0, 0))
    in_specs = [pl.BlockSpec((1, tile, ATT_DIM), cur)]
    args = [q]
    if window:
        in_specs += [pl.BlockSpec((1, ATT_BLOCK, kv_w), prev), pl.BlockSpec((1, tile, kv_w), cur),
                     pl.BlockSpec((1, ATT_BLOCK, kv_w), nxt)] * 2
        args += [k, k, k, v, v, v]
    in_specs += [ctx_spec, ctx_spec, pl.BlockSpec(memory_space=pltpu.SMEM)]
    args += [kx, vx, sink]
    return pl.pallas_call(
        functools.partial(_att_kernel, n_steps=n_steps, qb=qb, window=window),
        out_shape=jax.ShapeDtypeStruct((b_, t, ATT_DIM), F32),
        grid=(b_, n_steps),
        in_specs=in_specs,
        out_specs=pl.BlockSpec((1, tile, ATT_DIM), cur),
        compiler_params=_cparams("parallel", "parallel"),
        name="window_attn" if window else "ctx_attn",
    )(*args)


def _merge_kernel(x_ref, mod_ref, yc_ref, yl_ref, yg_ref, ya_ref, wgate_ref, pc_ref, pl_ref, pg_ref, pa_ref, wout_ref,
                  g_ref, b_ref, o_ref):
    x = x_ref[0]
    m = mod_ref[0]
    h = (_ln(x) * (1.0 + m[4:5]) + m[3:4]).astype(BF16)
    mrg = None
    for n, (y_ref, p_ref) in enumerate(((yc_ref, pc_ref), (yl_ref, pl_ref), (yg_ref, pg_ref), (ya_ref, pa_ref))):
        gate = jax.nn.sigmoid(_dot(h, wgate_ref[:, n * D_MODEL:(n + 1) * D_MODEL]))
        term = gate * _dot(y_ref[0].astype(BF16), p_ref[...])
        mrg = term if mrg is None else mrg + term
    out = _dot(mrg.astype(BF16), wout_ref[...])
    s = DEEPNORM_ALPHA * x + m[5:6] * out
    o_ref[0] = _ln(s) * g_ref[...] + b_ref[...]


def _merge(x, mod3, ys, wgate, projs, wout, ng, nb, *, ctx, tm):
    b_, t, d = x.shape
    weights = [wgate, *projs, wout, ng, nb]
    return pl.pallas_call(
        _merge_kernel,
        out_shape=jax.ShapeDtypeStruct(x.shape, F32),
        grid=(b_, t // tm),
        in_specs=[_tok_spec(tm, d), _mod_spec(ctx)] + [_tok_spec(tm, y.shape[-1]) for y in ys]
                 + [_resident(w.shape) for w in weights],
        out_specs=_tok_spec(tm, d),
        compiler_params=_cparams("parallel", "parallel"),
        name="merge_ctx" if ctx else "merge",
    )(x, mod3, *ys, *weights)


def _block_diag_halves(w):
    n, bd, _ = w.shape
    per = n // 2
    on_diag = jnp.eye(per, dtype=jnp.bool_)[None, :, None, :, None]
    out = jnp.where(on_diag, w.reshape(2, per, bd, 1, bd), 0.0)
    return out.reshape(2, per * bd, per * bd).astype(BF16)


def _rope_tables(t):
    pos = jnp.arange(t, dtype=jnp.int32)
    row = (pos // GRID_W).astype(F32)
    col = (pos % GRID_W).astype(F32)
    n_freq = ATT_HEAD_DIM // 4
    inv = ROPE_BASE ** (-jnp.arange(n_freq, dtype=F32) / n_freq)
    ang = jnp.concatenate([row[:, None] * inv, col[:, None] * inv], axis=-1)
    cos, sin = jnp.cos(ang), jnp.sin(ang)
    cos128 = jnp.tile(cos, (1, 4))
    sin128 = jnp.tile(jnp.concatenate([-sin, sin], axis=-1), (1, 2))
    return cos128, sin128


def _layer_params(l, ffn_w_in, ffn_w_out, w_in, lru_w_r, lru_w_i, proj_conv, proj_lru, proj_gdn, proj_att, w_out):
    ffn = [(ffn_w_in[l, f].astype(BF16), ffn_w_out[l, f].astype(BF16)) for f in range(2)]
    w = w_in[l]
    kd = ATT_HEAD_DIM
    wk = w[:, _OFF[9]:_OFF[10]]
    wv = w[:, _OFF[10]:_OFF[11]]
    dup = lambda m: jnp.concatenate([m[:, :kd], m[:, :kd], m[:, kd:], m[:, kd:]], axis=1)
    gab = jnp.pad(w[:, _OFF[6]:_OFF[8]], ((0, 0), (0, 128 - 4 * GDN_HEADS)))
    w_cat = jnp.concatenate([w[:, _OFF[0]:_OFF[2]], w[:, _OFF[2]:_OFF[4]], w[:, _OFF[4]:_OFF[6]], gab,
                             w[:, _OFF[8]:_OFF[9]], dup(wk), dup(wv)], axis=1).astype(BF16)
    wgate = w[:, _OFF[11]:_OFF[12]].astype(BF16)
    projs = [p[l].astype(BF16) for p in (proj_conv, proj_lru, proj_gdn, proj_att)]
    lru_w = [(_block_diag_halves(lru_w_r[l, d]), _block_diag_halves(lru_w_i[l, d])) for d in range(2)]
    return dict(ffn=ffn, w_cat=w_cat, wgate=wgate, projs=projs, wout=w_out[l].astype(BF16), lru_w=lru_w)


def kernel(x, c, ctx, c_ctx, ada_w, ada_b, norm_g, norm_b, ffn_w_in, ffn_w_out, w_in, conv_w, conv_b, conv_norm_g, conv_norm_b, lru_conv_w, lru_conv_b, lru_w_r, lru_b_r, lru_w_i, lru_b_i, lru_lam, gdn_conv_w, gdn_a_log, gdn_dt_bias, gdn_norm_g, att_sink, proj_conv, proj_lru, proj_gdn, proj_att, w_out):
    b_, t, d = x.shape
    lc = ctx.shape[1]
    assert d == D_MODEL and t % FFN_TILE == 0 and lc == 256 and b_ <= CTX_MOD_ROW
    c_all = jnp.zeros((MOD_ROWS, d), F32).at[:b_].set(c).at[CTX_MOD_ROW].set(c_ctx)
    mod_all = _modulation(c_all, ada_w, ada_b)
    rope_tabs = _rope_tables(t)
    row2 = lambda v: v.reshape(1, -1)
    xc = ctx
    for l in range(DEPTH):
        last = l == DEPTH - 1
        p = _layer_params(l, ffn_w_in, ffn_w_out, w_in, lru_w_r, lru_w_i, proj_conv, proj_lru, proj_gdn, proj_att, w_out)
        mod3 = mod_all[l]
        ng = [row2(norm_g[l, j]) for j in range(3)]
        nb = [row2(norm_b[l, j]) for j in range(3)]

        x = _ffn(x, mod3, *p["ffn"][0], ng[0], nb[0], j=0, ctx=False, tm=FFN_TILE)
        xc = _ffn(xc, mod3, *p["ffn"][0], ng[0], nb[0], j=0, ctx=True, tm=lc)

        z = _inproj(x, mod3, p["w_cat"], rope_tabs, ctx=False, tm=PROJ_TILE)
        zc = _inproj(xc, mod3, p["w_cat"], None, ctx=True, tm=lc)
        conv_in, lru_in, gdn_in, gab, q, k, v = z
        conv_c, lru_c, gdn_c, gab_c, q_c, k_c, v_c = zc

        conv_prm = (conv_w[l], row2(conv_b[l]), row2(conv_norm_g[l]), row2(conv_norm_b[l]))
        y_conv = _conformer(conv_in, *conv_prm, tt=SEQ_TILE)

        lru_prm = [(lru_conv_w[l, dd], row2(lru_conv_b[l, dd]), p["lru_w"][dd][0], row2(lru_b_r[l, dd]),
                    p["lru_w"][dd][1], row2(lru_b_i[l, dd]), row2(lru_lam[l, dd])) for dd in range(2)]
        h0 = jnp.zeros((b_, 1, LRU_DIM), F32)
        hc_f, st_f = _lru(lru_c, h0, lru_prm[0], None, tt=lc, reverse=False)
        hl_f, _ = _lru(lru_in, st_f, lru_prm[0], None, tt=LRU_TILE, reverse=False)
        y_lru_c, st_r = _lru(lru_c, h0, lru_prm[1], None if last else hc_f, tt=lc, reverse=True)
        y_lru, _ = _lru(lru_in, st_r, lru_prm[1], hl_f, tt=LRU_TILE, reverse=True)

        pad8 = lambda a: jnp.pad(a.reshape(1, -1), ((0, 0), (0, 128 - 2 * GDN_HEADS)))
        alog_row, dtb_row = pad8(gdn_a_log[l]), pad8(gdn_dt_bias[l])
        gdn_prm = [(gdn_conv_w[l, dd], alog_row, dtb_row) for dd in range(2)]
        gn = row2(gdn_norm_g[l])
        s0 = jnp.zeros((b_, GDN_HEADS, GDN_HEAD_DIM, GDN_HEAD_DIM), F32)
        oc_f, sg_f = _gdn(gdn_c, gab_c, s0, gdn_prm[0], None, None, tt=lc, direction=0)
        ol_f, _ = _gdn(gdn_in, gab, sg_f, gdn_prm[0], None, None, tt=GDN_TILE, direction=0)
        y_gdn_c, sg_r = _gdn(gdn_c, gab_c, s0, gdn_prm[1], None if last else oc_f, gn, tt=lc, direction=1)
        y_gdn, _ = _gdn(gdn_in, gab, sg_r, gdn_prm[1], ol_f, gn, tt=GDN_TILE, direction=1)

        y_att = _attention(q, k, v, k_c, v_c, att_sink[l], window=True)

        x = _merge(x, mod3, (y_conv, y_lru, y_gdn, y_att), p["wgate"], p["projs"], p["wout"], ng[1], nb[1],
                   ctx=False, tm=PROJ_TILE)
        x = _ffn(x, mod3, *p["ffn"][1], ng[2], nb[2], j=2, ctx=False, tm=FFN_TILE)
        if not last:
            y_conv_c = _conformer(conv_c, *conv_prm, tt=lc)
            y_att_c = _attention(q_c, None, None, k_c, v_c, att_sink[l], window=False)
            xc = _merge(xc, mod3, (y_conv_c, y_lru_c, y_gdn_c, y_att_c), p["wgate"], p["projs"], p["wout"],
                        ng[1], nb[1], ctx=True, tm=lc)
            xc = _ffn(xc, mod3, *p["ffn"][1], ng[2], nb[2], j=2, ctx=True, tm=lc)
    return x
```

```python
import functools

import numpy as np
import jax
import jax.numpy as jnp
from jax import lax
from jax.experimental import pallas as pl
from jax.experimental.pallas import tpu as pltpu

F32 = jnp.float32
BF16 = jnp.bfloat16

D_MODEL = 1024
DEPTH = 2
GRID_W = 64
CONV_DIM = 512
CONV_WIDTH = 31
LRU_DIM = 512
LRU_BLOCKS = 8
LRU_CONV = 4
LRU_C = 8.0
GDN_HEADS = 4
GDN_HEAD_DIM = 128
GDN_DIM = GDN_HEADS * GDN_HEAD_DIM
GDN_CONV = 4
GDN_CHUNK = 64
ATT_HEADS = 8
ATT_KV_HEADS = 2
ATT_HEAD_DIM = 64
ATT_DIM = ATT_HEADS * ATT_HEAD_DIM
ATT_BLOCK = 128
ROPE_BASE = 10000.0
N_BRANCH = 4
D_FF = 2816
LN_EPS = 1e-5
NORM_EPS = 1e-6
DEEPNORM_ALPHA = (2 * DEPTH) ** 0.25

CTX_MOD_ROW = 4
MOD_ROWS = 8
FF_CHUNKS = 11
SUBLANES = 8
CONV_HALO = 16
VMEM_LIMIT = 56 * 1024 * 1024
FFN_GROUP = 512
FFN_TILE = 2 * FFN_GROUP
PROJ_TILE = 512
SEQ_TILE = 512
GDN_TILE = 256
GDN_STAGE_RATES = (3, 2, 1)
ATT_QUERY_BLOCKS = 4
LRU_TILE = 1024
NEG_BIG = -1e30

_OFF = np.cumsum([0, CONV_DIM, CONV_DIM, LRU_DIM, LRU_DIM, 3 * GDN_DIM, GDN_DIM, 2 * GDN_HEADS, 2 * GDN_HEADS,
                  ATT_DIM, ATT_KV_HEADS * ATT_HEAD_DIM, ATT_KV_HEADS * ATT_HEAD_DIM, N_BRANCH * D_MODEL]).tolist()


def _cparams(*sem):
    return pltpu.CompilerParams(dimension_semantics=sem, vmem_limit_bytes=VMEM_LIMIT)


def _resident(shape):
    nd = len(shape)
    return pl.BlockSpec(tuple(shape), lambda *_: (0,) * nd, pipeline_mode=pl.Buffered(1))


def _ln(x):
    mu = jnp.mean(x, axis=-1, keepdims=True)
    xc = x - mu
    var = jnp.mean(xc * xc, axis=-1, keepdims=True)
    return xc * lax.rsqrt(var + LN_EPS)


def _silu(x):
    return x * jax.nn.sigmoid(x)


def _softplus(x):
    return jnp.maximum(x, 0.0) + jnp.log1p(jnp.exp(-jnp.abs(x)))


def _dot(a, b):
    return jnp.dot(a, b, preferred_element_type=F32)


def _dot_nt(a, b):
    return lax.dot_general(a, b, (((1,), (1,)), ((), ())), preferred_element_type=F32)


def _dot_tn(a, b):
    return lax.dot_general(a, b, (((0,), (0,)), ((), ())), preferred_element_type=F32)


def _split(a):
    hi = a.astype(BF16)
    return hi, (a - hi.astype(F32)).astype(BF16)


def _mod_spec(ctx):
    if ctx:
        return pl.BlockSpec((1, 9, D_MODEL), lambda b, i: (CTX_MOD_ROW, 0, 0))
    return pl.BlockSpec((1, 9, D_MODEL), lambda b, i: (b, 0, 0))


def _tok_spec(tm, width):
    return pl.BlockSpec((1, tm, width), lambda b, i: (b, i, 0))


def _mod_kernel(c_ref, w_ref, b_ref, o_ref):
    s = _silu(c_ref[...]).astype(BF16)
    o_ref[0] = _dot(s, w_ref[0].astype(BF16)) + b_ref[0]


def _modulation(c_all, ada_w, ada_b):
    n_l, d, n = ada_w.shape
    tn = 1024
    out = pl.pallas_call(
        _mod_kernel,
        out_shape=jax.ShapeDtypeStruct((n_l, MOD_ROWS, n), F32),
        grid=(n_l, n // tn),
        in_specs=[pl.BlockSpec((MOD_ROWS, d), lambda l, j: (0, 0)),
                  pl.BlockSpec((1, d, tn), lambda l, j: (l, 0, j)),
                  pl.BlockSpec((1, 1, tn), lambda l, j: (l, 0, j))],
        out_specs=pl.BlockSpec((1, MOD_ROWS, tn), lambda l, j: (l, 0, j)),
        compiler_params=_cparams("arbitrary", "arbitrary"),
        name="adaln_mod",
    )(c_all, ada_w, ada_b.reshape(n_l, 1, n))
    return out.reshape(n_l, MOD_ROWS, 9, d)


def _ffn_kernel(x_ref, mod_ref, wgu_ref, wd_ref, g_ref, b_ref, o_ref, *, j, n_sub):
    m = mod_ref[0]
    shift, scale, gate = m[3 * j:3 * j + 1], m[3 * j + 1:3 * j + 2], m[3 * j + 2:3 * j + 3]
    rows = x_ref.shape[1] // n_sub
    fc = D_FF // FF_CHUNKS
    for r in range(n_sub):
        rs = slice(r * rows, (r + 1) * rows)
        x = x_ref[0, rs]
        h = (_ln(x) * (1.0 + scale) + shift).astype(BF16)
        acc = None
        for f in range(FF_CHUNKS):
            g = _dot(h, wgu_ref[:, f * fc:(f + 1) * fc])
            u = _dot(h, wgu_ref[:, D_FF + f * fc:D_FF + (f + 1) * fc])
            a = (_silu(g) * u).astype(BF16)
            y = _dot(a, wd_ref[f * fc:(f + 1) * fc, :])
            acc = y if acc is None else acc + y
        s = DEEPNORM_ALPHA * x + (0.5 * gate) * acc
        o_ref[0, rs] = _ln(s) * g_ref[...] + b_ref[...]


def _ffn(x, mod3, wgu, wd, ng, nb, *, j, ctx, tm):
    b_, t, d = x.shape
    return pl.pallas_call(
        functools.partial(_ffn_kernel, j=j, n_sub=max(tm // FFN_GROUP, 1)),
        out_shape=jax.ShapeDtypeStruct(x.shape, F32),
        grid=(b_, t // tm),
        in_specs=[_tok_spec(tm, d), _mod_spec(ctx), _resident(wgu.shape), _resident(wd.shape),
                  _resident(ng.shape), _resident(nb.shape)],
        out_specs=_tok_spec(tm, d),
        compiler_params=_cparams("parallel", "parallel"),
        name="ffn_ctx" if ctx else "ffn",
    )(x, mod3, wgu, wd, ng, nb)


def _rope128(t, cos, sin_signed, lane_lo):
    swapped = jnp.where(lane_lo, pltpu.roll(t, 96, 1), pltpu.roll(t, 32, 1))
    return t * cos + swapped * sin_signed


def _inproj_kernel(*refs, rope):
    if rope:
        x_ref, mod_ref, w_ref, cos_ref, sin_ref, o_conv, o_lru, o_gdn, o_gab, o_q, o_k, o_v = refs
    else:
        x_ref, mod_ref, w_ref, o_conv, o_lru, o_gdn, o_gab, o_q, o_k, o_v = refs
    m = mod_ref[0]
    h = (_ln(x_ref[0]) * (1.0 + m[4:5]) + m[3:4]).astype(BF16)
    o_conv[0] = _dot(h, w_ref[:, 0:1024])
    o_lru[0] = _dot(h, w_ref[:, 1024:2048])
    o_gdn[0] = _dot(h, w_ref[:, 2048:4096])
    o_gab[0] = _dot(h, w_ref[:, 4096:4224])
    q = _dot(h, w_ref[:, 4224:4736])
    k = _dot(h, w_ref[:, 4736:4992])
    o_v[0] = _dot(h, w_ref[:, 4992:5248])
    scale = ATT_HEAD_DIM ** -0.5
    if rope:
        cos, sin = cos_ref[...], sin_ref[...]
        lane_lo = (lax.broadcasted_iota(jnp.int32, cos.shape, 1) % ATT_HEAD_DIM) < (ATT_HEAD_DIM // 2)
        for s in range(ATT_DIM // 128):
            o_q[0, :, 128 * s:128 * (s + 1)] = _rope128(q[:, 128 * s:128 * (s + 1)], cos, sin, lane_lo) * scale
        for s in range(2):
            o_k[0, :, 128 * s:128 * (s + 1)] = _rope128(k[:, 128 * s:128 * (s + 1)], cos, sin, lane_lo)
    else:
        o_q[0] = q * scale
        o_k[0] = k


_INPROJ_WIDTHS = (1024, 1024, 2048, 128, 512, 256, 256)


def _inproj(x, mod3, w_cat, rope_tabs, *, ctx, tm):
    b_, t, d = x.shape
    in_specs = [_tok_spec(tm, d), _mod_spec(ctx), _resident(w_cat.shape)]
    args = [x, mod3, w_cat]
    if rope_tabs is not None:
        in_specs += [pl.BlockSpec((tm, 128), lambda b, i: (i, 0))] * 2
        args += list(rope_tabs)
    return pl.pallas_call(
        functools.partial(_inproj_kernel, rope=rope_tabs is not None),
        out_shape=[jax.ShapeDtypeStruct((b_, t, w), F32) for w in _INPROJ_WIDTHS],
        grid=(b_, t // tm),
        in_specs=in_specs,
        out_specs=[_tok_spec(tm, w) for w in _INPROJ_WIDTHS],
        compiler_params=_cparams("parallel", "parallel"),
        name="inproj_ctx" if ctx else "inproj",
    )(*args)


def _conv_kernel(main_ref, prev_ref, next_ref, w_ref, cb_ref, g_ref, b_ref, o_ref, ubuf, *, tt, n_t, rb):
    i = pl.program_id(1)

    def glu(blk):
        return blk[:, :CONV_DIM] * jax.nn.sigmoid(blk[:, CONV_DIM:])

    n_rows = tt + 2 * CONV_HALO
    ubuf[0, 0:CONV_HALO] = jnp.where(i > 0, glu(prev_ref[0]), 0.0)
    ubuf[0, CONV_HALO:CONV_HALO + tt] = glu(main_ref[0])
    ubuf[0, CONV_HALO + tt:] = jnp.where(i < n_t - 1, glu(next_ref[0]), 0.0)
    for s in range(1, SUBLANES):
        ubuf[s, 0:n_rows - SUBLANES] = ubuf[0, s:s + n_rows - SUBLANES]
    base = CONV_HALO - (CONV_WIDTH - 1) // 2
    for r0 in range(0, tt, rb):
        acc = None
        for k in range(CONV_WIDTH):
            tile, s = divmod(base + k, SUBLANES)
            start = r0 + tile * SUBLANES
            term = ubuf[s, start:start + rb, :] * w_ref[k]
            acc = term if acc is None else acc + term
        y = _ln(acc + cb_ref[...]) * g_ref[...] + b_ref[...]
        o_ref[0, r0:r0 + rb, :] = _silu(y)


def _conformer(conv_in, w, cb, ng, nb, *, tt):
    b_, t, width = conv_in.shape
    n_t = t // tt
    hb = tt // CONV_HALO
    n_h = t // CONV_HALO
    rb = 2 * SUBLANES
    w = jnp.broadcast_to(w[:, None, :], (CONV_WIDTH, rb, CONV_DIM))
    return pl.pallas_call(
        functools.partial(_conv_kernel, tt=tt, n_t=n_t, rb=rb),
        out_shape=jax.ShapeDtypeStruct((b_, t, CONV_DIM), F32),
        grid=(b_, n_t),
        in_specs=[_tok_spec(tt, width),
                  pl.BlockSpec((1, CONV_HALO, width), lambda b, i: (b, jnp.maximum(i * hb - 1, 0), 0)),
                  pl.BlockSpec((1, CONV_HALO, width), lambda b, i: (b, jnp.minimum((i + 1) * hb, n_h - 1), 0)),
                  _resident(w.shape), _resident(cb.shape), _resident(ng.shape), _resident(nb.shape)],
        out_specs=_tok_spec(tt, CONV_DIM),
        scratch_shapes=[pltpu.VMEM((SUBLANES, tt + 2 * CONV_HALO, CONV_DIM), F32)],
        compiler_params=_cparams("parallel", "parallel"),
        name="conformer_conv",
    )(conv_in, conv_in, conv_in, w, cb, ng, nb)


def _halo_specs(tt, t, width, reverse):
    n_t = t // tt
    hb = tt // SUBLANES
    n_h = t // SUBLANES
    if reverse:
        main = pl.BlockSpec((1, tt, width), lambda b, i: (b, n_t - 1 - i, 0))
        halo = pl.BlockSpec((1, SUBLANES, width), lambda b, i: (b, jnp.minimum((n_t - i) * hb, n_h - 1), 0))
    else:
        main = pl.BlockSpec((1, tt, width), lambda b, i: (b, i, 0))
        halo = pl.BlockSpec((1, SUBLANES, width), lambda b, i: (b, jnp.maximum(i * hb - 1, 0), 0))
    return main, halo


def _fill_conv_buffer(xbuf, main, halo, first, tt, reverse):
    halo = jnp.where(first, 0.0, halo)
    if reverse:
        xbuf[0:tt] = main
        xbuf[tt:] = halo
        return lambda k: LRU_CONV - 1 - k
    xbuf[0:SUBLANES] = halo
    xbuf[SUBLANES:] = main
    return lambda k: SUBLANES - (LRU_CONV - 1) + k


def _lru_kernel(*refs, tt, reverse, final):
    if final:
        (main_ref, halo_ref, h0_ref, cw_ref, cb_ref, wr_ref, br_ref, wi_ref, bi_ref, lam_ref, hf_ref,
         o_ref, st_ref, xbuf, abuf, bbuf, carry) = refs
    else:
        (main_ref, halo_ref, h0_ref, cw_ref, cb_ref, wr_ref, br_ref, wi_ref, bi_ref, lam_ref,
         o_ref, st_ref, xbuf, abuf, bbuf, carry) = refs
    i = pl.program_id(1)

    @pl.when(i == 0)
    def _():
        carry[...] = h0_ref[0]

    off = _fill_conv_buffer(xbuf, main_ref[0, :, 0:LRU_DIM], halo_ref[0, :, 0:LRU_DIM], i == 0, tt, reverse)
    xc = cb_ref[...]
    for k in range(LRU_CONV):
        xc = xc + xbuf[off(k):off(k) + tt, :] * cw_ref[k:k + 1, :]
    xh = xc.astype(BF16)
    half = LRU_DIM // 2
    r_lin = jnp.concatenate([_dot(xh[:, :half], wr_ref[0]), _dot(xh[:, half:], wr_ref[1])], axis=1)
    i_lin = jnp.concatenate([_dot(xh[:, :half], wi_ref[0]), _dot(xh[:, half:], wi_ref[1])], axis=1)
    r = jax.nn.sigmoid(r_lin + br_ref[...])
    gi = jax.nn.sigmoid(i_lin + bi_ref[...])
    log_a = (-LRU_C) * r * _softplus(-lam_ref[...])
    a = jnp.exp(log_a)
    th = jnp.tanh(log_a)
    bb = jnp.sqrt(-2.0 * th / (1.0 - th)) * (gi * xc)

    n_g = tt // SUBLANES
    a3 = a.reshape(n_g, SUBLANES, LRU_DIM)
    b3 = bb.reshape(n_g, SUBLANES, LRU_DIM)
    row = lax.broadcasted_iota(jnp.int32, a3.shape, 1)
    for s in (1, 2, 4):
        if reverse:
            keep = row < SUBLANES - s
            a_sh = pltpu.roll(a3, SUBLANES - s, 1)
            b_sh = pltpu.roll(b3, SUBLANES - s, 1)
        else:
            keep = row >= s
            a_sh = pltpu.roll(a3, s, 1)
            b_sh = pltpu.roll(b3, s, 1)
        b3 = jnp.where(keep, a3 * b_sh + b3, b3)
        a3 = jnp.where(keep, a3 * a_sh, a3)
    abuf[...] = a3.reshape(tt, LRU_DIM)
    bbuf[...] = b3.reshape(tt, LRU_DIM)

    hprev = carry[...]
    order = range(n_g - 1, -1, -1) if reverse else range(n_g)
    edge = 0 if reverse else SUBLANES - 1
    for g in order:
        rows = slice(g * SUBLANES, (g + 1) * SUBLANES)
        hg = abuf[rows, :] * hprev + bbuf[rows, :]
        hprev = hg[edge:edge + 1, :]
        if final:
            gate = jax.nn.gelu(main_ref[0, rows, LRU_DIM:2 * LRU_DIM])
            o_ref[0, rows, :] = (hf_ref[0, rows, :] + hg) * gate
        else:
            o_ref[0, rows, :] = hg
    carry[...] = hprev
    st_ref[0] = hprev


def _lru(lru_in, h0, prm, h_fwd, *, tt, reverse):
    b_, t, width = lru_in.shape
    final = h_fwd is not None
    main, halo = _halo_specs(tt, t, width, reverse)
    in_specs = [main, halo, pl.BlockSpec((1, 1, LRU_DIM), lambda b, i: (b, 0, 0))] + [_resident(p.shape) for p in prm]
    args = [lru_in, lru_in, h0] + list(prm)
    out_tile = pl.BlockSpec((1, tt, LRU_DIM), main.index_map)
    if final:
        in_specs.append(out_tile)
        args.append(h_fwd)
    return pl.pallas_call(
        functools.partial(_lru_kernel, tt=tt, reverse=reverse, final=final),
        out_shape=[jax.ShapeDtypeStruct((b_, t, LRU_DIM), F32), jax.ShapeDtypeStruct((b_, 1, LRU_DIM), F32)],
        grid=(b_, t // tt),
        in_specs=in_specs,
        out_specs=[out_tile, pl.BlockSpec((1, 1, LRU_DIM), lambda b, i: (b, 0, 0))],
        scratch_shapes=[pltpu.VMEM((tt + SUBLANES, LRU_DIM), F32), pltpu.VMEM((tt, LRU_DIM), F32),
                        pltpu.VMEM((tt, LRU_DIM), F32), pltpu.VMEM((1, LRU_DIM), F32)],
        compiler_params=_cparams("parallel", "arbitrary"),
        name="rglru_rev" if reverse else "rglru_fwd",
    )(*args)


def _interleave(*streams):
    live = list(streams)
    while live:
        still = []
        for gen, rate in live:
            for _ in range(rate):
                if next(gen, StopIteration) is StopIteration:
                    break
            else:
                still.append((gen, rate))
        live = still


def _unit_tri_inverses(ms, ri, ci, eye, out):
    def same_block(log2_size):
        return jnp.right_shift(ri, log2_size) == jnp.right_shift(ci, log2_size)

    base = same_block(3)
    ps = [jnp.where(base, -m, 0.0) for m in ms]
    xs = [eye + p for p in ps]
    for _ in range(2):
        p16 = [p.astype(BF16) for p in ps]
        ps = [_dot(p, p) for p in p16]
        yield
        xs = [x + _dot(x.astype(BF16), p.astype(BF16)) for x, p in zip(xs, ps)]
        yield
    for log2_size in (3, 4, 5):
        level = same_block(log2_size + 1) & ~same_block(log2_size)
        x16 = [x.astype(BF16) for x in xs]
        ys = [_dot(x, jnp.where(level, m, 0.0).astype(BF16)) for x, m in zip(x16, ms)]
        yield
        xs = [x - _dot(y.astype(BF16), xb) for x, y, xb in zip(xs, ys, x16)]
        yield
    out.extend(xs)


def _gdn_kernel(*refs, tt, nb, direction, final):
    if final:
        (main_ref, halo_ref, gab_ref, s0_ref, cw_ref, alog_ref, dtb_ref, of_ref, gn_ref,
         o_ref, sfin_ref, xbuf, s_ref) = refs
    else:
        (main_ref, halo_ref, gab_ref, s0_ref, cw_ref, alog_ref, dtb_ref,
         o_ref, sfin_ref, xbuf, s_ref) = refs
    reverse = direction == 1
    i = pl.program_id(0)
    c_ = GDN_CHUNK
    hd = GDN_HEAD_DIM
    qkv_w = 3 * GDN_DIM

    @pl.when(i == 0)
    def _():
        s_ref[...] = s0_ref[...]

    for b in range(nb):
        off = _fill_conv_buffer(xbuf.at[b], main_ref[b, :, 0:qkv_w], halo_ref[b, :, 0:qkv_w], i == 0, tt, reverse)

    ri = lax.broadcasted_iota(jnp.int32, (c_, c_), 0)
    ci = lax.broadcasted_iota(jnp.int32, (c_, c_), 1)
    incl = (ci >= ri) if reverse else (ci <= ri)
    strict = (ci > ri) if reverse else (ci < ri)
    eye = (ri == ci).astype(F32)
    ones_tri = jnp.where(incl, 1.0, 0.0).astype(BF16)
    last = 0 if reverse else c_ - 1

    n_c = tt // c_
    chunks = list(range(n_c - 1, -1, -1) if reverse else range(n_c))
    heads = range(GDN_HEADS)

    batch = range(nb)
    lanes = [(b, h) for b in batch for h in heads]

    def elementwise_stage(c, pre):
        r0 = c * c_
        for b in batch:
            acc = None
            for k in range(GDN_CONV):
                term = xbuf[b, r0 + off(k):r0 + off(k) + c_, :] * cw_ref[k:k + 1, :]
                acc = term if acc is None else acc + term
            u = _silu(acc)
            yield
            ga = gab_ref[b, r0:r0 + c_, :]
            g_all = -jnp.exp(alog_ref[...]) * _softplus(ga + dtb_ref[...])
            beta_all = jax.nn.sigmoid(ga)
            g_hi, g_lo = _split(g_all)
            g_lo2 = (g_all - g_hi.astype(F32) - g_lo.astype(F32)).astype(BF16)
            gcum = _dot(ones_tri, g_hi) + (_dot(ones_tri, g_lo) + _dot(ones_tri, g_lo2))
            gcum_t = gcum.T
            g_last = gcum[last:last + 1, :]
            for h in heads:
                col = direction * GDN_HEADS + h
                q, k, v = (u[:, part * GDN_DIM + h * hd:part * GDN_DIM + (h + 1) * hd] for part in range(3))
                q = q * (lax.rsqrt(jnp.sum(q * q, axis=-1, keepdims=True) + NORM_EPS) * (hd ** -0.5))
                k = k * lax.rsqrt(jnp.sum(k * k, axis=-1, keepdims=True) + NORM_EPS)
                gc = gcum[:, col:col + 1]
                gl = g_last[:, col:col + 1]
                beta = beta_all[:, 2 * GDN_HEADS + col:2 * GDN_HEADS + col + 1]
                decay = jnp.exp(jnp.where(incl, gc - gcum_t[col:col + 1, :], -jnp.inf))
                kb = k * beta
                k16 = k.astype(BF16)
                eg = jnp.exp(gc)
                pre[b, h] = dict(
                    m=jnp.where(strict, _dot_nt(kb.astype(BF16), k16) * decay, 0.0),
                    attn=(_dot_nt(q.astype(BF16), k16) * decay).astype(BF16),
                    rhs=jnp.concatenate([v * beta, kb * eg], axis=1).astype(BF16),
                    qd=q * eg, ks=(k * jnp.exp(gl - gc)).astype(BF16), cd=jnp.exp(gl))
                yield

    def solve_stage(pre):
        t_inv = []
        yield from _unit_tri_inverses([pre[bh]["m"] for bh in lanes], ri, ci, eye, t_inv)
        for bh, t in zip(lanes, t_inv):
            sol = _dot(t.astype(BF16), pre[bh]["rhs"])
            pre[bh]["u"] = sol[:, :hd]
            pre[bh]["wq"] = jnp.concatenate([sol[:, hd:], pre[bh]["qd"]], axis=0).astype(BF16)
            if bh[1] == GDN_HEADS - 1:
                yield

    def recurrence_stage(c, cur, state):
        r0 = c * c_
        wq = {bh: _dot(cur[bh]["wq"], state[bh].astype(BF16)) for bh in lanes}
        yield
        vn16 = {bh: (cur[bh]["u"] - wq[bh][:c_]).astype(BF16) for bh in lanes}
        outs = {bh: wq[bh][c_:] + _dot(cur[bh]["attn"], vn16[bh]) for bh in lanes}
        yield
        state.update({bh: state[bh] * cur[bh]["cd"] + _dot_tn(cur[bh]["ks"], vn16[bh]) for bh in lanes})
        yield
        for b, h in lanes:
            cols = slice(h * hd, (h + 1) * hd)
            o = outs[b, h]
            if final:
                o = o + of_ref[b, r0:r0 + c_, cols]
                z = main_ref[b, r0:r0 + c_, qkv_w + h * hd:qkv_w + (h + 1) * hd]
                y = o * lax.rsqrt(jnp.mean(o * o, axis=-1, keepdims=True) + NORM_EPS) * gn_ref[...]
                o_ref[b, r0:r0 + c_, cols] = y * _silu(z)
            else:
                o_ref[b, r0:r0 + c_, cols] = o
            if h == GDN_HEADS - 1:
                yield

    state = {bh: s_ref[bh] for bh in lanes}
    pre = {n: {} for n in range(n_c)}
    for n in range(n_c + 2):
        streams = []
        if n < n_c:
            streams.append((elementwise_stage(chunks[n], pre[n]), GDN_STAGE_RATES[0]))
        if 1 <= n <= n_c:
            streams.append((solve_stage(pre[n - 1]), GDN_STAGE_RATES[1]))
        if n >= 2:
            streams.append((recurrence_stage(chunks[n - 2], pre.pop(n - 2), state), GDN_STAGE_RATES[2]))
        _interleave(*streams)
    for bh in lanes:
        s_ref[bh] = state[bh]
    sfin_ref[...] = s_ref[...]


def _gdn(gdn_in, gab, s0, prm, o_fwd, gn, *, tt, direction):
    b_, t, width = gdn_in.shape
    final = o_fwd is not None
    reverse = direction == 1
    n_t = t // tt
    hb = tt // SUBLANES
    n_h = t // SUBLANES
    if reverse:
        tile = lambda i: (0, n_t - 1 - i, 0)
        halo_map = lambda i: (0, jnp.minimum((n_t - i) * hb, n_h - 1), 0)
    else:
        tile = lambda i: (0, i, 0)
        halo_map = lambda i: (0, jnp.maximum(i * hb - 1, 0), 0)
    st_shape = (b_, GDN_HEADS, GDN_HEAD_DIM, GDN_HEAD_DIM)
    st_spec = pl.BlockSpec(st_shape, lambda i: (0, 0, 0, 0))
    out_tile = pl.BlockSpec((b_, tt, GDN_DIM), tile)
    in_specs = [pl.BlockSpec((b_, tt, width), tile), pl.BlockSpec((b_, SUBLANES, width), halo_map),
                pl.BlockSpec((b_, tt, 128), tile), st_spec] + [_resident(p.shape) for p in prm]
    args = [gdn_in, gdn_in, gab, s0] + list(prm)
    if final:
        in_specs += [out_tile, _resident(gn.shape)]
        args += [o_fwd, gn]
    return pl.pallas_call(
        functools.partial(_gdn_kernel, tt=tt, nb=b_, direction=direction, final=final),
        out_shape=[jax.ShapeDtypeStruct((b_, t, GDN_DIM), F32), jax.ShapeDtypeStruct(st_shape, F32)],
        grid=(n_t,),
        in_specs=in_specs,
        out_specs=[out_tile, st_spec],
        scratch_shapes=[pltpu.VMEM((b_, tt + SUBLANES, 3 * GDN_DIM), F32), pltpu.VMEM(st_shape, F32)],
        compiler_params=_cparams("arbitrary"),
        name="gdn_rev" if reverse else "gdn_fwd",
    )(*args)


def _att_kernel(*refs, n_steps, qb, window):
    if window:
        q_ref, kp_ref, kc_ref, kn_ref, vp_ref, vc_ref, vn_ref, kx_ref, vx_ref, sink_ref, o_ref = refs
    else:
        q_ref, kx_ref, vx_ref, sink_ref, o_ref = refs
    i = pl.program_id(1)
    blk = ATT_BLOCK
    group = ATT_HEADS // ATT_KV_HEADS
    lane = lax.broadcasted_iota(jnp.int32, (blk, 128), 1)
    lo = lane < ATT_HEAD_DIM
    n_ctx = kx_ref.shape[1]
    row_head = lax.broadcasted_iota(jnp.int32, (group * blk, 1), 0) // blk
    if window:
        r = lax.broadcasted_iota(jnp.int32, (group * blk, 3 * blk + n_ctx), 0) & (blk - 1)
        c = lax.broadcasted_iota(jnp.int32, (group * blk, 3 * blk + n_ctx), 1)
        inner = ((c >= blk) | (c >= r)) & ((c < 2 * blk) | (c >= 3 * blk) | (c - 2 * blk <= r))
        masks = []
        for sb in range(qb):
            m = inner
            if sb == 0:
                m = m & (c >= jnp.where(i > 0, 0, blk))
            if sb == qb - 1:
                m = m & ((c < 2 * blk) | (c >= jnp.where(i < n_steps - 1, 2 * blk, 3 * blk)))
            masks.append(m)
    chains = [(sb, j) for sb in range(qb) for j in range(ATT_KV_HEADS)]
    v_alls, scores, sinks = [], [], []
    for sb, j in chains:
        cols = slice(128 * j, 128 * (j + 1))
        if window:
            k_win = jnp.concatenate([kp_ref[0, :, cols], kc_ref[0, :, cols], kn_ref[0, :, cols]], axis=0)
            v_win = jnp.concatenate([vp_ref[0, :, cols], vc_ref[0, :, cols], vn_ref[0, :, cols]], axis=0)
            rows = slice(sb * blk, (sb + 3) * blk)
            k_all = jnp.concatenate([k_win[rows], kx_ref[0, :, cols]], axis=0)
            v_all = jnp.concatenate([v_win[rows], vx_ref[0, :, cols]], axis=0)
        else:
            k_all, v_all = kx_ref[0, :, cols], vx_ref[0, :, cols]
        v_alls.append(v_all.astype(BF16))
        q_rows = []
        sk = jnp.zeros((group * blk, 1), F32)
        for e in range(group):
            head = group * j + e
            qp = q_ref[0, sb * blk:(sb + 1) * blk, 128 * (head // 2):128 * (head // 2 + 1)]
            q_rows.append(jnp.where(lo if head % 2 == 0 else ~lo, qp, 0.0).astype(BF16))
            sk = jnp.where(row_head == e, sink_ref[head], sk)
        sinks.append(sk)
        s = _dot_nt(jnp.concatenate(q_rows, axis=0), k_all.astype(BF16))
        scores.append(jnp.where(masks[sb], s, NEG_BIG) if window else s)
    mxs = [jnp.maximum(jnp.max(s, axis=-1, keepdims=True), sk) for s, sk in zip(scores, sinks)]
    ps = [jnp.exp(s - mx) for s, mx in zip(scores, mxs)]
    dens = [jnp.sum(p, axis=-1, keepdims=True) + jnp.exp(sk - mx) for p, sk, mx in zip(ps, sinks, mxs)]
    outs = [_dot(p.astype(BF16), v_all) / den for p, v_all, den in zip(ps, v_alls, dens)]
    for (sb, j), o in zip(chains, outs):
        for pair in range(group // 2):
            slab = (group * j) // 2 + pair
            o_ref[0, sb * blk:(sb + 1) * blk, 128 * slab:128 * (slab + 1)] = jnp.where(
                lo, o[2 * pair * blk:(2 * pair + 1) * blk], o[(2 * pair + 1) * blk:(2 * pair + 2) * blk])


def _attention(q, k, v, kx, vx, sink, *, window):
    b_, t, _ = q.shape
    qb = min(ATT_QUERY_BLOCKS, t // ATT_BLOCK)
    tile = qb * ATT_BLOCK
    n_steps = t // tile
    n_blk = t // ATT_BLOCK
    n_ctx = kx.shape[1]
    cur = lambda b, i: (b, i, 0)
    prev = lambda b, i: (b, jnp.maximum(i * qb - 1, 0), 0)
    nxt = lambda b, i: (b, jnp.minimum((i + 1) * qb, n_blk - 1), 0)
    kv_w = kx.shape[-1]
    ctx_spec = pl.BlockSpec((1, n_ctx, kv_w), lambda b, i: (b, 0, 0))
    in_specs = [pl.BlockSpec((1, tile, ATT_DIM), cur)]
    args = [q]
    if window:
        in_specs += [pl.BlockSpec((1, ATT_BLOCK, kv_w), prev), pl.BlockSpec((1, tile, kv_w), cur),
                     pl.BlockSpec((1, ATT_BLOCK, kv_w), nxt)] * 2
        args += [k, k, k, v, v, v]
    in_specs += [ctx_spec, ctx_spec, pl.BlockSpec(memory_space=pltpu.SMEM)]
    args += [kx, vx, sink]
    return pl.pallas_call(
        functools.partial(_att_kernel, n_steps=n_steps, qb=qb, window=window),
        out_shape=jax.ShapeDtypeStruct((b_, t, ATT_DIM), F32),
        grid=(b_, n_steps),
        in_specs=in_specs,
        out_specs=pl.BlockSpec((1, tile, ATT_DIM), cur),
        compiler_params=_cparams("parallel", "parallel"),
        name="window_attn" if window else "ctx_attn",
    )(*args)


def _merge_kernel(x_ref, mod_ref, yc_ref, yl_ref, yg_ref, ya_ref, wgate_ref, pc_ref, pl_ref, pg_ref, pa_ref, wout_ref,
                  g_ref, b_ref, o_ref):
    x = x_ref[0]
    m = mod_ref[0]
    h = (_ln(x) * (1.0 + m[4:5]) + m[3:4]).astype(BF16)
    mrg = None
    for n, (y_ref, p_ref) in enumerate(((yc_ref, pc_ref), (yl_ref, pl_ref), (yg_ref, pg_ref), (ya_ref, pa_ref))):
        gate = jax.nn.sigmoid(_dot(h, wgate_ref[:, n * D_MODEL:(n + 1) * D_MODEL]))
        term = gate * _dot(y_ref[0].astype(BF16), p_ref[...])
        mrg = term if mrg is None else mrg + term
    out = _dot(mrg.astype(BF16), wout_ref[...])
    s = DEEPNORM_ALPHA * x + m[5:6] * out
    o_ref[0] = _ln(s) * g_ref[...] + b_ref[...]


def _merge(x, mod3, ys, wgate, projs, wout, ng, nb, *, ctx, tm):
    b_, t, d = x.shape
    weights = [wgate, *projs, wout, ng, nb]
    return pl.pallas_call(
        _merge_kernel,
        out_shape=jax.ShapeDtypeStruct(x.shape, F32),
        grid=(b_, t // tm),
        in_specs=[_tok_spec(tm, d), _mod_spec(ctx)] + [_tok_spec(tm, y.shape[-1]) for y in ys]
                 + [_resident(w.shape) for w in weights],
        out_specs=_tok_spec(tm, d),
        compiler_params=_cparams("parallel", "parallel"),
        name="merge_ctx" if ctx else "merge",
    )(x, mod3, *ys, *weights)


def _block_diag_halves(w):
    n, bd, _ = w.shape
    per = n // 2
    on_diag = jnp.eye(per, dtype=jnp.bool_)[None, :, None, :, None]
    out = jnp.where(on_diag, w.reshape(2, per, bd, 1, bd), 0.0)
    return out.reshape(2, per * bd, per * bd).astype(BF16)


def _rope_tables(t):
    pos = jnp.arange(t, dtype=jnp.int32)
    row = (pos // GRID_W).astype(F32)
    col = (pos % GRID_W).astype(F32)
    n_freq = ATT_HEAD_DIM // 4
    inv = ROPE_BASE ** (-jnp.arange(n_freq, dtype=F32) / n_freq)
    ang = jnp.concatenate([row[:, None] * inv, col[:, None] * inv], axis=-1)
    cos, sin = jnp.cos(ang), jnp.sin(ang)
    cos128 = jnp.tile(cos, (1, 4))
    sin128 = jnp.tile(jnp.concatenate([-sin, sin], axis=-1), (1, 2))
    return cos128, sin128


def _layer_params(l, ffn_w_in, ffn_w_out, w_in, lru_w_r, lru_w_i, proj_conv, proj_lru, proj_gdn, proj_att, w_out):
    ffn = [(ffn_w_in[l, f].astype(BF16), ffn_w_out[l, f].astype(BF16)) for f in range(2)]
    w = w_in[l]
    kd = ATT_HEAD_DIM
    wk = w[:, _OFF[9]:_OFF[10]]
    wv = w[:, _OFF[10]:_OFF[11]]
    dup = lambda m: jnp.concatenate([m[:, :kd], m[:, :kd], m[:, kd:], m[:, kd:]], axis=1)
    gab = jnp.pad(w[:, _OFF[6]:_OFF[8]], ((0, 0), (0, 128 - 4 * GDN_HEADS)))
    w_cat = jnp.concatenate([w[:, _OFF[0]:_OFF[2]], w[:, _OFF[2]:_OFF[4]], w[:, _OFF[4]:_OFF[6]], gab,
                             w[:, _OFF[8]:_OFF[9]], dup(wk), dup(wv)], axis=1).astype(BF16)
    wgate = w[:, _OFF[11]:_OFF[12]].astype(BF16)
    projs = [p[l].astype(BF16) for p in (proj_conv, proj_lru, proj_gdn, proj_att)]
    lru_w = [(_block_diag_halves(lru_w_r[l, d]), _block_diag_halves(lru_w_i[l, d])) for d in range(2)]
    return dict(ffn=ffn, w_cat=w_cat, wgate=wgate, projs=projs, wout=w_out[l].astype(BF16), lru_w=lru_w)


def kernel(x, c, ctx, c_ctx, ada_w, ada_b, norm_g, norm_b, ffn_w_in, ffn_w_out, w_in, conv_w, conv_b, conv_norm_g, conv_norm_b, lru_conv_w, lru_conv_b, lru_w_r, lru_b_r, lru_w_i, lru_b_i, lru_lam, gdn_conv_w, gdn_a_log, gdn_dt_bias, gdn_norm_g, att_sink, proj_conv, proj_lru, proj_gdn, proj_att, w_out):
    b_, t, d = x.shape
    lc = ctx.shape[1]
    assert d == D_MODEL and t % FFN_TILE == 0 and lc == 256 and b_ <= CTX_MOD_ROW
    c_all = jnp.zeros((MOD_ROWS, d), F32).at[:b_].set(c).at[CTX_MOD_ROW].set(c_ctx)
    mod_all = _modulation(c_all, ada_w, ada_b)
    rope_tabs = _rope_tables(t)
    row2 = lambda v: v.reshape(1, -1)
    flat = lambda a: a.reshape(1, b_ * lc, a.shape[-1])
    unflat = lambda a: a.reshape(b_, lc, a.shape[-1])
    xc = ctx
    for l in range(DEPTH):
        last = l == DEPTH - 1
        p = _layer_params(l, ffn_w_in, ffn_w_out, w_in, lru_w_r, lru_w_i, proj_conv, proj_lru, proj_gdn, proj_att, w_out)
        mod3 = mod_all[l]
        ng = [row2(norm_g[l, j]) for j in range(3)]
        nb = [row2(norm_b[l, j]) for j in range(3)]

        x = _ffn(x, mod3, *p["ffn"][0], ng[0], nb[0], j=0, ctx=False, tm=FFN_TILE)
        xc = unflat(_ffn(flat(xc), mod3, *p["ffn"][0], ng[0], nb[0], j=0, ctx=True, tm=min(FFN_TILE, b_ * lc)))

        z = _inproj(x, mod3, p["w_cat"], rope_tabs, ctx=False, tm=PROJ_TILE)
        zc = [unflat(o) for o in _inproj(flat(xc), mod3, p["w_cat"], None, ctx=True, tm=min(PROJ_TILE, b_ * lc))]
        conv_in, lru_in, gdn_in, gab, q, k, v = z
        conv_c, lru_c, gdn_c, gab_c, q_c, k_c, v_c = zc

        conv_prm = (conv_w[l], row2(conv_b[l]), row2(conv_norm_g[l]), row2(conv_norm_b[l]))
        y_conv = _conformer(conv_in, *conv_prm, tt=SEQ_TILE)

        lru_prm = [(lru_conv_w[l, dd], row2(lru_conv_b[l, dd]), p["lru_w"][dd][0], row2(lru_b_r[l, dd]),
                    p["lru_w"][dd][1], row2(lru_b_i[l, dd]), row2(lru_lam[l, dd])) for dd in range(2)]
        h0 = jnp.zeros((b_, 1, LRU_DIM), F32)
        hc_f, st_f = _lru(lru_c, h0, lru_prm[0], None, tt=lc, reverse=False)
        hl_f, _ = _lru(lru_in, st_f, lru_prm[0], None, tt=LRU_TILE, reverse=False)
        y_lru_c, st_r = _lru(lru_c, h0, lru_prm[1], None if last else hc_f, tt=lc, reverse=True)
        y_lru, _ = _lru(lru_in, st_r, lru_prm[1], hl_f, tt=LRU_TILE, reverse=True)

        pad8 = lambda a: jnp.pad(a.reshape(1, -1), ((0, 0), (0, 128 - 2 * GDN_HEADS)))
        alog_row, dtb_row = pad8(gdn_a_log[l]), pad8(gdn_dt_bias[l])
        gdn_prm = [(gdn_conv_w[l, dd], alog_row, dtb_row) for dd in range(2)]
        gn = row2(gdn_norm_g[l])
        s0 = jnp.zeros((b_, GDN_HEADS, GDN_HEAD_DIM, GDN_HEAD_DIM), F32)
        oc_f, sg_f = _gdn(gdn_c, gab_c, s0, gdn_prm[0], None, None, tt=lc, direction=0)
        ol_f, _ = _gdn(gdn_in, gab, sg_f, gdn_prm[0], None, None, tt=GDN_TILE, direction=0)
        y_gdn_c, sg_r = _gdn(gdn_c, gab_c, s0, gdn_prm[1], None if last else oc_f, gn, tt=lc, direction=1)
        y_gdn, _ = _gdn(gdn_in, gab, sg_r, gdn_prm[1], ol_f, gn, tt=GDN_TILE, direction=1)

        y_att = _attention(q, k, v, k_c, v_c, att_sink[l], window=True)

        x = _merge(x, mod3, (y_conv, y_lru, y_gdn, y_att), p["wgate"], p["projs"], p["wout"], ng[1], nb[1],
                   ctx=False, tm=PROJ_TILE)
        x = _ffn(x, mod3, *p["ffn"][1], ng[2], nb[2], j=2, ctx=False, tm=FFN_TILE)
        if not last:
            y_conv_c = _conformer(conv_c, *conv_prm, tt=lc)
            y_att_c = _attention(q_c, None, None, k_c, v_c, att_sink[l], window=False)
            ys_c = tuple(flat(y) for y in (y_conv_c, y_lru_c, y_gdn_c, y_att_c))
            xc = _merge(flat(xc), mod3, ys_c, p["wgate"], p["projs"], p["wout"],
                        ng[1], nb[1], ctx=True, tm=min(PROJ_TILE, b_ * lc))
            xc = unflat(_ffn(xc, mod3, *p["ffn"][1], ng[2], nb[2], j=2, ctx=True, tm=min(FFN_TILE, b_ * lc)))
    return x
```
